```python
import jax, jax.numpy as jnp
from jax import lax
import numpy as np

D_MODEL = 1024
BATCH = 8
SEQ = 8192
DEPTH = 4

N_MEM = 256
HEAD_DIM = 64
GROUP_W = D_MODEL // 4
N_HEADS = GROUP_W // HEAD_DIM
D_MIX = 4 * GROUP_W
GLA_DK = HEAD_DIM // 2
GLA_DV = HEAD_DIM
GLA_QK = N_HEADS * GLA_DK
GLA_V = N_HEADS * GLA_DV
GLA_GATE_RANK = 16
GLA_TAU = 16.0
CONV_W = 4
CHUNK = 64
GLA_COLS = 2 * GLA_QK + GLA_V + GLA_GATE_RANK + GLA_V
HG_DK = HEAD_DIM
HG_DV = HEAD_DIM
HG_K = N_HEADS * HG_DK
HG_V = N_HEADS * HG_DV
HG_COLS = 2 * HG_K + 2 * HG_V
RW_W = N_HEADS * HEAD_DIM
RW_DECAY_RANK = 64
RW_A_RANK = 64
RW_GATE_RANK = 128
RW_LN_EPS = 64e-5
RW_COLS = 3 * RW_W + RW_DECAY_RANK + RW_A_RANK + RW_GATE_RANK
MEM_HEADS = 4
MEM_W = MEM_HEADS * HEAD_DIM
P_IN = GLA_COLS + HG_COLS + RW_COLS + MEM_W
N_EXPERTS = 32
TOP_K = 4
D_EXPERT = D_MODEL
SWIGLU_ALPHA = 1.702
SWIGLU_LIMIT = 7.0
MOE_BLOCK = 128
DN_ALPHA = (2.0 * DEPTH) ** 0.25
DN_BETA = (8.0 * DEPTH) ** -0.25
LN_EPS = 1e-5
RMS_EPS = 1e-6
GATE_FLOOR = 1e-30

kernel_name = 'hybrid_gla_hgrn2_rwkv7_mem_moe_deepnorm'


def _layer_norm(x, g, b):
    xf = x.astype(jnp.float32)
    mu = jnp.mean(xf, -1, keepdims=True)
    var = jnp.mean(jnp.square(xf - mu), -1, keepdims=True)
    return ((xf - mu) * lax.rsqrt(var + LN_EPS) * g + b).astype(x.dtype)


def _head_rms(o, g):
    B, T, H, d = o.shape
    o = o * lax.rsqrt(jnp.mean(jnp.square(o), -1, keepdims=True) + RMS_EPS)
    return o.reshape(B, T, H * d) * g


def _causal_dwconv(u, w):
    C = u.shape[-1]
    return lax.conv_general_dilated(u, w[:, None, :].astype(u.dtype), window_strides=(1,),
                                    padding=[(CONV_W - 1, 0)],
                                    dimension_numbers=('NWC', 'WIO', 'NWC'),
                                    feature_group_count=C)


def _chunk_gated_linear_attention(q, k, v, log_g):
    B, T, H, dk = q.shape
    dv = v.shape[-1]
    n = T // CHUNK

    def to_chunks(a):
        return a.astype(jnp.float32).reshape(B, n, CHUNK, H, a.shape[-1]).transpose(1, 0, 3, 2, 4)

    qc, kc, vc, gc = (to_chunks(a) for a in (q, k, v, log_g))
    bc = jnp.cumsum(gc, axis=3)
    causal = jnp.tril(jnp.ones((CHUNK, CHUNK), bool))[:, :, None]

    def step(S, inp):
        q_, k_, v_, b_ = inp
        rel = b_[:, :, :, None, :] - b_[:, :, None, :, :]
        dec = jnp.where(causal, jnp.exp(jnp.where(causal, rel, 0.0)), 0.0)
        att = jnp.einsum('bhid,bhjd,bhijd->bhij', q_, k_, dec)
        o = jnp.einsum('bhij,bhjv->bhiv', att, v_) + jnp.einsum('bhid,bhdv->bhiv', q_ * jnp.exp(b_), S)
        b_end = b_[:, :, -1:, :]
        S = S * jnp.exp(b_end)[:, :, 0, :, None] + jnp.einsum('bhjd,bhjv->bhdv', k_ * jnp.exp(b_end - b_), v_)
        return S, o

    S0 = jnp.zeros((B, H, dk, dv), jnp.float32)
    _, o = lax.scan(step, S0, (qc, kc, vc, bc))
    return o.transpose(1, 0, 3, 2, 4).reshape(B, T, H, dv)


def _rwkv7_scan(r, w, k, v, a, b):
    B, T, H, N = r.shape

    def step(S, inp):
        r_t, w_t, k_t, v_t, a_t, b_t = inp
        sa = jnp.einsum('bhij,bhj->bhi', S, a_t)
        S = S * w_t[:, :, None, :] + sa[..., None] * b_t[:, :, None, :] + v_t[..., None] * k_t[:, :, None, :]
        return S, jnp.einsum('bhij,bhj->bhi', S, r_t)

    xs = tuple(jnp.swapaxes(t, 0, 1) for t in (r, w, k, v, a, b))
    _, y = lax.scan(step, jnp.zeros((B, H, N, N), jnp.float32), xs)
    return jnp.swapaxes(y, 0, 1)


def _mixer(x, mem, w_in, gla_conv, gla_gate_w2, gla_gate_b, gla_norm_g, hg_lb, hg_norm_g,
           rw_mu, rw_w0, rw_w2, rw_a0, rw_a2, rw_g2, rw_k_k, rw_k_a, rw_r_k, rw_ln_w, rw_ln_b,
           w_mem_k, w_mem_v, w_out):
    B, T, _ = x.shape
    H = N_HEADS
    f32 = jnp.float32
    p = (x @ w_in).astype(f32)
    p_gla, p_hg, p_rw, q_mem = jnp.split(p, [GLA_COLS, GLA_COLS + HG_COLS, GLA_COLS + HG_COLS + RW_COLS], axis=-1)

    qkv = jax.nn.silu(_causal_dwconv(p_gla[..., :2 * GLA_QK + GLA_V], gla_conv))
    gq, gk, gv = jnp.split(qkv, [GLA_QK, 2 * GLA_QK], axis=-1)
    g_lr, g_out = jnp.split(p_gla[..., 2 * GLA_QK + GLA_V:], [GLA_GATE_RANK], axis=-1)
    log_alpha = jax.nn.log_sigmoid(g_lr @ gla_gate_w2 + gla_gate_b) / GLA_TAU
    o = _chunk_gated_linear_attention(gq.reshape(B, T, H, GLA_DK) * GLA_DK ** -0.5,
                                      gk.reshape(B, T, H, GLA_DK),
                                      gv.reshape(B, T, H, GLA_DV),
                                      log_alpha.reshape(B, T, H, GLA_DK))
    o_gla = _head_rms(o, gla_norm_g) * jax.nn.silu(g_out)

    hq, hf, hi, hgate = jnp.split(p_hg, [HG_K, 2 * HG_K, 2 * HG_K + HG_V], axis=-1)
    f_gate = hg_lb + (1.0 - hg_lb) * jax.nn.sigmoid(hf)
    log_f = jnp.log(jnp.maximum(f_gate, GATE_FLOOR))
    k_in = (1.0 - hg_lb) * jax.nn.sigmoid(-hf)
    o = _chunk_gated_linear_attention(jax.nn.silu(hq).reshape(B, T, H, HG_DK),
                                      k_in.reshape(B, T, H, HG_DK),
                                      hi.reshape(B, T, H, HG_DV),
                                      log_f.reshape(B, T, H, HG_DK))
    o_hg = _head_rms(o, hg_norm_g) * jax.nn.silu(hgate)

    prev = jnp.pad(p_rw, ((0, 0), (1, 0), (0, 0)))[:, :-1]
    p_rw = p_rw + (prev - p_rw) * rw_mu
    rr, rk, rv, w_lr, a_lr, g_lr2 = jnp.split(
        p_rw, np.cumsum([RW_W, RW_W, RW_W, RW_DECAY_RANK, RW_A_RANK]).tolist(), axis=-1)
    w_log = -jax.nn.softplus(-(rw_w0 + jnp.tanh(w_lr) @ rw_w2)) - 0.5
    decay = jnp.exp(-jnp.exp(w_log))
    a = jax.nn.sigmoid(rw_a0 + a_lr @ rw_a2)
    g = jax.nn.sigmoid(g_lr2) @ rw_g2
    heads = lambda t: t.reshape(B, T, H, HEAD_DIM)
    kk = heads(rk * rw_k_k)
    kk = kk / jnp.maximum(jnp.sqrt(jnp.sum(jnp.square(kk), -1, keepdims=True)), 1e-12)
    hk = heads(rk * (1.0 + (a - 1.0) * rw_k_a))
    hr, hv, ha = heads(rr), heads(rv), heads(a)
    y = _rwkv7_scan(hr, heads(decay), hk, hv, -kk, kk * ha)
    mu = jnp.mean(y, -1, keepdims=True)
    var = jnp.mean(jnp.square(y - mu), -1, keepdims=True)
    y = ((y - mu) * lax.rsqrt(var + RW_LN_EPS)).reshape(B, T, RW_W) * rw_ln_w + rw_ln_b
    bonus = (jnp.sum(hr * hk * rw_r_k, -1, keepdims=True) * hv).reshape(B, T, RW_W)
    o_rw = (y + bonus) * g

    M = mem.shape[1]
    mk = (mem @ w_mem_k).astype(f32).reshape(B, M, MEM_HEADS, HEAD_DIM)
    mv = (mem @ w_mem_v).astype(f32).reshape(B, M, MEM_HEADS, HEAD_DIM)
    s = jnp.einsum('bthd,bmhd->bhtm', q_mem.reshape(B, T, MEM_HEADS, HEAD_DIM), mk) * HEAD_DIM ** -0.5
    o_mem = jnp.einsum('bhtm,bmhd->bthd', jax.nn.softmax(s, axis=-1), mv).reshape(B, T, MEM_W)

    o = jnp.concatenate([o_gla, o_hg, o_rw, o_mem], axis=-1).astype(x.dtype)
    return o @ w_out


def _moe(x, router_w, router_b, w_gu, b_gu, w_down, b_down):
    B, T, D = x.shape
    n_tok = B * T
    h = x.reshape(n_tok, D)
    logits = (h @ router_w + router_b).astype(jnp.float32)
    top_val, top_idx = lax.top_k(logits, TOP_K)
    gates = jax.nn.softmax(top_val, axis=-1)
    n_assign = n_tok * TOP_K
    flat_e = top_idx.reshape(-1).astype(jnp.int32)
    flat_tok = jnp.arange(n_assign, dtype=jnp.int32) // TOP_K
    order = jnp.argsort(flat_e)
    sorted_e = flat_e[order]
    sorted_tok = flat_tok[order]
    sorted_gate = gates.reshape(-1)[order]
    counts = jnp.bincount(flat_e, length=N_EXPERTS).astype(jnp.int32)
    padded = (counts + MOE_BLOCK - 1) // MOE_BLOCK * MOE_BLOCK
    start = jnp.cumsum(counts) - counts
    pad_end = jnp.cumsum(padded)
    pad_start = pad_end - padded
    dest = pad_start[sorted_e] + jnp.arange(n_assign, dtype=jnp.int32) - start[sorted_e]
    n_blocks = -(-n_assign // MOE_BLOCK) + N_EXPERTS
    slot_tok = jnp.full((n_blocks * MOE_BLOCK,), n_tok, jnp.int32).at[dest].set(sorted_tok)
    h_pad = jnp.concatenate([h, jnp.zeros((1, D), h.dtype)], axis=0)
    xb = h_pad[slot_tok].reshape(n_blocks, MOE_BLOCK, D)
    block_e = jnp.minimum(jnp.searchsorted(pad_end, jnp.arange(n_blocks, dtype=jnp.int32) * MOE_BLOCK, side='right'),
                          N_EXPERTS - 1)

    def expert_block(args):
        xe, e = args
        hh = xe @ w_gu[e] + b_gu[e]
        gate = jnp.minimum(hh[:, ::2], SWIGLU_LIMIT)
        up = jnp.clip(hh[:, 1::2], -SWIGLU_LIMIT, SWIGLU_LIMIT)
        glu = gate * jax.nn.sigmoid(gate * SWIGLU_ALPHA)
        return ((up + 1.0) * glu) @ w_down[e] + b_down[e]

    yb = lax.map(expert_block, (xb, block_e)).reshape(n_blocks * MOE_BLOCK, D)
    y_sorted = yb[dest] * sorted_gate[:, None].astype(yb.dtype)
    y = jax.ops.segment_sum(y_sorted, sorted_tok, num_segments=n_tok)
    return y.reshape(B, T, D).astype(x.dtype)


def setup_inputs(seed: int = 0) -> dict:
    key = jax.random.key(seed)
    keys = jax.random.split(key, 40)
    L, D, E, F = DEPTH, D_MODEL, N_EXPERTS, D_EXPERT

    def nrm(i, shape, scale):
        return jax.random.normal(keys[i], shape, jnp.float32) * scale

    def unif(i, shape, lo, hi):
        return jax.random.uniform(keys[i], shape, jnp.float32, lo, hi)

    return {
        'x': nrm(0, (BATCH, SEQ, D), 1.0),
        'mem': nrm(1, (BATCH, N_MEM, D), 1.0),
        'w_in': nrm(2, (L, D, P_IN), D ** -0.5),
        'gla_conv': nrm(3, (L, CONV_W, 2 * GLA_QK + GLA_V), CONV_W ** -0.5),
        'gla_gate_w2': nrm(4, (L, GLA_GATE_RANK, GLA_QK), GLA_GATE_RANK ** -0.5),
        'gla_gate_b': nrm(5, (L, GLA_QK), 0.1),
        'gla_norm_g': 1.0 + nrm(6, (L, GLA_V), 0.02),
        'hg_lower_bound': 1.0 + nrm(7, (L, HG_K), 0.1),
        'hg_norm_g': 1.0 + nrm(8, (L, HG_V), 0.02),
        'rw_mu': unif(9, (L, RW_COLS), 0.0, 1.0),
        'rw_w0': unif(10, (L, RW_W), -5.0, -1.0),
        'rw_w2': nrm(11, (L, RW_DECAY_RANK, RW_W), 0.5 * RW_DECAY_RANK ** -0.5),
        'rw_a0': nrm(12, (L, RW_W), 0.5),
        'rw_a2': nrm(13, (L, RW_A_RANK, RW_W), 0.5 * RW_A_RANK ** -0.5),
        'rw_g2': nrm(14, (L, RW_GATE_RANK, RW_W), RW_GATE_RANK ** -0.5),
        'rw_k_k': 0.85 + nrm(15, (L, RW_W), 0.05),
        'rw_k_a': 1.0 + nrm(16, (L, RW_W), 0.05),
        'rw_r_k': nrm(17, (L, N_HEADS, HEAD_DIM), 0.1),
        'rw_ln_w': 1.0 + nrm(18, (L, RW_W), 0.02),
        'rw_ln_b': nrm(19, (L, RW_W), 0.02),
        'w_mem_k': nrm(20, (L, D, MEM_W), D ** -0.5),
        'w_mem_v': nrm(21, (L, D, MEM_W), D ** -0.5),
        'w_out': nrm(22, (L, D_MIX, D), DN_BETA * D_MIX ** -0.5),
        'ln1_g': 1.0 + nrm(23, (L, D), 0.02),
        'ln1_b': nrm(24, (L, D), 0.02),
        'router_w': nrm(25, (L, D, E), D ** -0.5),
        'router_b': nrm(26, (L, E), 0.01),
        'w_gu': nrm(27, (L, E, D, 2 * F), D ** -0.5),
        'b_gu': nrm(28, (L, E, 2 * F), 0.01),
        'w_down': nrm(29, (L, E, F, D), DN_BETA * F ** -0.5),
        'b_down': nrm(30, (L, E, D), 0.01),
        'ln2_g': 1.0 + nrm(31, (L, D), 0.02),
        'ln2_b': nrm(32, (L, D), 0.02),
    }


def reference(x, mem, w_in, gla_conv, gla_gate_w2, gla_gate_b, gla_norm_g, hg_lower_bound, hg_norm_g,
              rw_mu, rw_w0, rw_w2, rw_a0, rw_a2, rw_g2, rw_k_k, rw_k_a, rw_r_k, rw_ln_w, rw_ln_b,
              w_mem_k, w_mem_v, w_out, ln1_g, ln1_b, router_w, router_b, w_gu, b_gu, w_down, b_down,
              ln2_g, ln2_b):
    lb = jax.nn.softmax(hg_lower_bound.astype(jnp.float32), axis=0)
    lb = jnp.cumsum(lb, axis=0) - lb[0]
    for l in range(DEPTH):
        h = _mixer(x, mem, w_in[l], gla_conv[l], gla_gate_w2[l], gla_gate_b[l], gla_norm_g[l], lb[l], hg_norm_g[l],
                   rw_mu[l], rw_w0[l], rw_w2[l], rw_a0[l], rw_a2[l], rw_g2[l], rw_k_k[l], rw_k_a[l], rw_r_k[l],
                   rw_ln_w[l], rw_ln_b[l], w_mem_k[l], w_mem_v[l], w_out[l])
        x = _layer_norm(DN_ALPHA * x + h, ln1_g[l], ln1_b[l])
        m = _moe(x, router_w[l], router_b[l], w_gu[l], b_gu[l], w_down[l], b_down[l])
        x = _layer_norm(DN_ALPHA * x + m, ln2_g[l], ln2_b[l])
    return x
```

```python
import functools

import numpy as np
import jax
import jax.numpy as jnp
from jax import lax
from jax.experimental import pallas as pl
from jax.experimental.pallas import tpu as pltpu

F32 = jnp.float32
BF16 = jnp.bfloat16
HIGHEST = lax.Precision.HIGHEST

D_MODEL = 1024
HEAD_DIM = 64
N_HEADS = 4
GROUP_W = 256
GLA_DK = 32
GLA_QK = 128
GLA_V = 256
GLA_GATE_RANK = 16
GLA_TAU = 16.0
CONV_W = 4
GLA_CONV_C = 2 * GLA_QK + GLA_V
GLA_COLS = 784
GLA_PAD = 896
HG_COLS = 1024
RW_W = 256
RW_DECAY_RANK = 64
RW_A_RANK = 64
RW_GATE_RANK = 128
RW_COLS = 1024
RW_LN_EPS = 64e-5
MEM_W = 256
N_EXPERTS = 32
TOP_K = 4
SWIGLU_ALPHA = 1.702
SWIGLU_LIMIT = 7.0
LN_EPS = 1e-5
RMS_EPS = 1e-6
GATE_FLOOR = 1e-30

LANES = 128
SUBLANES = 8
SUB_CHUNK = 16
RW_CHUNK = 64
VMEM_LIMIT = 48 * 1024 * 1024


def _cparams(n_axes):
    return pltpu.CompilerParams(dimension_semantics=("arbitrary",) * n_axes, vmem_limit_bytes=VMEM_LIMIT)


def _dot(a, b, precision=None):
    return jnp.dot(a, b, preferred_element_type=F32, precision=precision)


def _dot_nt(a, b):
    return lax.dot_general(a, b, (((1,), (1,)), ((), ())), preferred_element_type=F32)


def _dot_tn(a, b):
    return lax.dot_general(a, b, (((0,), (0,)), ((), ())), preferred_element_type=F32)


def _sigmoid(x):
    return 1.0 / (1.0 + jnp.exp(-x))


def _silu(x):
    return x * _sigmoid(x)


def _softplus(x):
    return jnp.maximum(x, 0.0) + jnp.log(1.0 + jnp.exp(-jnp.abs(x)))


def _const_spec(shape):
    nd = len(shape)
    return pl.BlockSpec(shape, lambda *_: (0,) * nd)


def _proj_kernel(x_ref, w_ref, *out_refs, widths):
    xb = x_ref[...].astype(BF16)
    off = 0
    for o_ref, wd in zip(out_refs, widths):
        o_ref[...] = _dot(xb, w_ref[:, off:off + wd]).astype(o_ref.dtype)
        off += wd


def _project(x2d, w_bf16, widths, tm):
    m, k = x2d.shape
    n = w_bf16.shape[1]
    assert sum(widths) == n and m % tm == 0
    return pl.pallas_call(
        functools.partial(_proj_kernel, widths=tuple(widths)),
        grid=(m // tm,),
        in_specs=[pl.BlockSpec((tm, k), lambda i: (i, 0)), _const_spec((k, n))],
        out_specs=[pl.BlockSpec((tm, wd), lambda i: (i, 0)) for wd in widths],
        out_shape=[jax.ShapeDtypeStruct((m, wd), F32) for wd in widths],
        compiler_params=_cparams(1),
        name="in_proj",
    )(x2d, w_bf16)


def _gla_core(q, k, v, g, lt_ref, ekv_ref, bdt_ref, st_ref):
    tt, ck = q.shape
    cv = v.shape[1]
    nc = tt // SUB_CHUNK
    b = _dot(lt_ref[...], g, precision=HIGHEST)
    q3 = q.reshape(nc, SUB_CHUNK, ck)
    k3 = k.reshape(nc, SUB_CHUNK, ck)
    v3 = v.reshape(nc, SUB_CHUNK, cv)
    b3 = b.reshape(nc, SUB_CHUNK, ck)
    row = lax.broadcasted_iota(jnp.int32, (nc, SUB_CHUNK, cv), 1)
    ekv = ekv_ref[...]

    o3 = jnp.zeros((nc, SUB_CHUNK, cv), F32)
    for j in range(SUB_CHUNK):
        rel = jnp.minimum(b3 - b3[:, j:j + 1, :], 0.0)
        w = q3 * jnp.exp(rel) * k3[:, j:j + 1, :]
        s = _dot(w.reshape(tt, ck).astype(BF16), ekv).reshape(nc, SUB_CHUNK, cv)
        o3 = o3 + jnp.where(row >= j, s, 0.0) * v3[:, j:j + 1, :]

    bend3 = b3[:, SUB_CHUNK - 1:SUB_CHUNK, :]
    qg = (q3 * jnp.exp(b3)).reshape(tt, ck).astype(BF16)
    kg = (k3 * jnp.exp(bend3 - b3)).reshape(tt, ck).astype(BF16)
    dec = jnp.exp(bend3)
    vb = v.astype(BF16)
    bdt = bdt_ref[...]
    st = st_ref[...]
    outs = []
    for c in range(nc):
        sl = slice(c * SUB_CHUNK, (c + 1) * SUB_CHUNK)
        outs.append(_dot_nt(qg[sl], st.astype(BF16)))
        st = st * dec[c] + bdt * _dot_tn(vb[sl], kg[sl])
    st_ref[...] = st
    return o3.reshape(tt, cv) + jnp.concatenate(outs, axis=0)


def _head_rms_gate(o, evv, norm_g, gate):
    ms = _dot((o * o).astype(BF16), evv) * (1.0 / HEAD_DIM)
    return o * lax.rsqrt(ms + RMS_EPS) * norm_g * _silu(gate)


def _gla_kernel(p_ref, conv_ref, w2_ref, gb_ref, ng_ref, lt_ref, ekv_ref, bdt_ref, evv_ref,
                o_ref, xp_ref, st_ref, *, tt):
    t = pl.program_id(1)

    @pl.when(t == 0)
    def _():
        st_ref[...] = jnp.zeros_like(st_ref)
        xp_ref[tt:tt + SUBLANES, :] = jnp.zeros((SUBLANES, GLA_CONV_C), F32)

    xp_ref[0:SUBLANES, :] = xp_ref[tt:tt + SUBLANES, :]
    xp_ref[SUBLANES:tt + SUBLANES, :] = p_ref[:, 0:GLA_CONV_C]
    acc = jnp.zeros((tt, GLA_CONV_C), F32)
    for i in range(CONV_W):
        shift = CONV_W - 1 - i
        acc = acc + xp_ref[SUBLANES - shift:SUBLANES - shift + tt, :] * conv_ref[i:i + 1, :]
    qkv = _silu(acc)
    q = qkv[:, 0:GLA_QK] * (GLA_DK ** -0.5)
    k = qkv[:, GLA_QK:2 * GLA_QK]
    v = qkv[:, 2 * GLA_QK:GLA_CONV_C]
    g_out = p_ref[:, GLA_CONV_C:GLA_CONV_C + GLA_V]
    z = _dot(p_ref[:, GLA_CONV_C + GLA_V:GLA_PAD], w2_ref[...], precision=HIGHEST) + gb_ref[...]
    log_alpha = -_softplus(-z) * (1.0 / GLA_TAU)
    o = _gla_core(q, k, v, log_alpha, lt_ref, ekv_ref, bdt_ref, st_ref)
    o_ref[...] = _head_rms_gate(o, evv_ref[...], ng_ref[...], g_out).astype(o_ref.dtype)


def _hgrn_kernel(p_ref, lb_ref, ng_ref, lt_ref, ekv_ref, bdt_ref, evv_ref, o_ref, st_ref):
    t = pl.program_id(1)

    @pl.when(t == 0)
    def _():
        st_ref[...] = jnp.zeros_like(st_ref)

    hq = p_ref[:, 0:256]
    hf = p_ref[:, 256:512]
    hi = p_ref[:, 512:768]
    hgate = p_ref[:, 768:1024]
    lb = lb_ref[...]
    f_gate = lb + (1.0 - lb) * _sigmoid(hf)
    log_f = jnp.log(jnp.maximum(f_gate, GATE_FLOOR))
    k_in = (1.0 - lb) * _sigmoid(-hf)
    o = _gla_core(_silu(hq), k_in, hi, log_f, lt_ref, ekv_ref, bdt_ref, st_ref)
    o_ref[...] = _head_rms_gate(o, evv_ref[...], ng_ref[...], hgate).astype(o_ref.dtype)


def _head_of(n, width):
    return np.arange(n) // width


def _gla_consts(tt, ck, cv):
    r = np.arange(tt)
    lt = ((r[:, None] // SUB_CHUNK == r[None, :] // SUB_CHUNK) & (r[None, :] <= r[:, None])).astype(np.float32)
    hk = _head_of(ck, ck // N_HEADS)
    hv = _head_of(cv, cv // N_HEADS)
    ekv = (hk[:, None] == hv[None, :]).astype(np.float32)
    evv = (hv[:, None] == hv[None, :]).astype(np.float32)
    return (jnp.asarray(lt, F32), jnp.asarray(ekv, BF16), jnp.asarray(ekv.T, F32), jnp.asarray(evv, BF16))


def _gla_group(p_gla, conv_w, w2pad, gate_b, norm_g, batch, seq, tt):
    lt, ekv, bdt, evv = _gla_consts(tt, GLA_QK, GLA_V)
    nt = seq // tt
    row_spec = lambda w: pl.BlockSpec((tt, w), lambda b, t: (b * nt + t, 0))
    return pl.pallas_call(
        functools.partial(_gla_kernel, tt=tt),
        grid=(batch, nt),
        in_specs=[row_spec(GLA_PAD), _const_spec(conv_w.shape), _const_spec(w2pad.shape),
                  _const_spec(gate_b.shape), _const_spec(norm_g.shape), _const_spec(lt.shape),
                  _const_spec(ekv.shape), _const_spec(bdt.shape), _const_spec(evv.shape)],
        out_specs=row_spec(GLA_V),
        out_shape=jax.ShapeDtypeStruct((batch * seq, GLA_V), BF16),
        scratch_shapes=[pltpu.VMEM((tt + 2 * SUBLANES, GLA_CONV_C), F32), pltpu.VMEM((GLA_V, GLA_QK), F32)],
        compiler_params=_cparams(2),
        name="gla_group",
    )(p_gla, conv_w, w2pad, gate_b, norm_g, lt, ekv, bdt, evv)


def _hgrn_group(p_hg, lb, norm_g, batch, seq, tt):
    lt, ekv, bdt, evv = _gla_consts(tt, 256, 256)
    nt = seq // tt
    row_spec = lambda w: pl.BlockSpec((tt, w), lambda b, t: (b * nt + t, 0))
    return pl.pallas_call(
        _hgrn_kernel,
        grid=(batch, nt),
        in_specs=[row_spec(HG_COLS), _const_spec(lb.shape), _const_spec(norm_g.shape), _const_spec(lt.shape),
                  _const_spec(ekv.shape), _const_spec(bdt.shape), _const_spec(evv.shape)],
        out_specs=row_spec(256),
        out_shape=jax.ShapeDtypeStruct((batch * seq, 256), BF16),
        scratch_shapes=[pltpu.VMEM((256, 256), F32)],
        compiler_params=_cparams(2),
        name="hgrn_group",
    )(p_hg, lb, norm_g, lt, ekv, bdt, evv)


def _expand_heads(x, hm):
    return jnp.concatenate([x * hm[h:h + 1, :] for h in range(N_HEADS)], axis=0)


def _fold_heads(x_e, c):
    out = x_e[0:c]
    for h in range(1, N_HEADS):
        out = out + x_e[h * c:(h + 1) * c]
    return out


def _rwkv_kernel(p_ref, mu_ref, w0_ref, w2_ref, a0_ref, a2_ref, g2_ref, kk_ref, ka_ref, rk_ref,
                 lnw_ref, lnb_ref, tri_ref, hm_ref, sl_ref, il_ref, eye_ref, bd_ref, evv_ref,
                 o_ref, xp_ref, st_ref, *, tt):
    t = pl.program_id(1)

    @pl.when(t == 0)
    def _():
        st_ref[...] = jnp.zeros_like(st_ref)
        xp_ref[tt:tt + SUBLANES, :] = jnp.zeros((SUBLANES, RW_COLS), F32)

    xp_ref[0:SUBLANES, :] = xp_ref[tt:tt + SUBLANES, :]
    xp_ref[SUBLANES:tt + SUBLANES, :] = p_ref[...]
    cur = p_ref[...]
    prev = xp_ref[SUBLANES - 1:SUBLANES - 1 + tt, :]
    p = cur + (prev - cur) * mu_ref[...]
    rr = p[:, 0:256]
    rk = p[:, 256:512]
    rv = p[:, 512:768]
    lr = p[:, 768:896]
    g_lr = p[:, 896:1024]
    w_log = -_softplus(-(w0_ref[...] + _dot(jnp.tanh(lr), w2_ref[...], precision=HIGHEST))) - 0.5
    lw = -jnp.exp(w_log)
    a = _sigmoid(a0_ref[...] + _dot(lr, a2_ref[...], precision=HIGHEST))
    g = _dot(_sigmoid(g_lr).astype(BF16), g2_ref[...])
    evv = evv_ref[...]
    kk = rk * kk_ref[...]
    kk_sq = _dot(kk * kk, evv.astype(F32), precision=HIGHEST)
    kk = kk / jnp.maximum(jnp.sqrt(kk_sq), 1e-12)
    hk = rk * (1.0 + (a - 1.0) * ka_ref[...])
    a_vec = -kk
    b_vec = kk * a

    hm = hm_ref[...]
    sl_m = sl_ref[...]
    il_m = il_ref[...]
    bd = bd_ref[...]
    eye = eye_ref[...]
    tri = tri_ref[...]
    c = RW_CHUNK
    st = st_ref[...]
    ys = []
    for ci in range(tt // c):
        s_ = slice(ci * c, (ci + 1) * c)
        lw_c = lw[s_]
        cs = _dot(tri, lw_c, precision=HIGHEST)
        gam = jnp.exp(cs)
        igam = jnp.exp(-cs)
        gam_end = jnp.exp(cs[c - 1:c, :])
        at = a_vec[s_] * jnp.exp(cs - lw_c)
        bh = b_vec[s_] * igam
        kh = hk[s_] * igam
        rt = rr[s_] * gam
        v_c = rv[s_]
        at_e = _expand_heads(at, hm).astype(BF16)
        bh_e = _expand_heads(bh, hm).astype(BF16)
        kh_e = _expand_heads(kh, hm).astype(BF16)
        rt_e = _expand_heads(rt, hm).astype(BF16)
        v_e = _expand_heads(v_c, hm).astype(BF16)
        l_ab = _dot_nt(at_e, bh_e) * sl_m
        l_ak = _dot_nt(at_e, kh_e) * sl_m
        l_rb = _dot_nt(rt_e, bh_e) * il_m
        l_rk = _dot_nt(rt_e, kh_e) * il_m
        tm = eye + l_ab
        pw = l_ab
        for _ in range(int(np.log2(c)) - 1):
            pwb = pw.astype(BF16)
            pw = _dot(pwb, pwb)
            tm = tm + _dot(tm.astype(BF16), pw.astype(BF16))
        tmb = tm.astype(BF16)
        w_e = _dot(tmb, at_e)
        uv_e = _dot(tmb, _dot(l_ak.astype(BF16), v_e).astype(BF16))
        stb = st.astype(BF16)
        u_e = _dot_nt(w_e.astype(BF16), stb) + uv_e
        y_e = _dot_nt(rt_e, stb) + _dot(l_rb.astype(BF16), u_e.astype(BF16)) + _dot(l_rk.astype(BF16), v_e)
        ys.append(_fold_heads(y_e, c))
        u = _fold_heads(u_e, c)
        lhs = jnp.concatenate([u, v_c], axis=0).astype(BF16)
        rhs = jnp.concatenate([bh * gam_end, kh * gam_end], axis=0).astype(BF16)
        st = st * gam_end + bd * _dot_tn(lhs, rhs)
    st_ref[...] = st
    y = jnp.concatenate(ys, axis=0)

    inv_n = 1.0 / HEAD_DIM
    mu = _dot(y.astype(BF16), evv) * inv_n
    yc = y - mu
    var = _dot((yc * yc).astype(BF16), evv) * inv_n
    y = yc * lax.rsqrt(var + RW_LN_EPS) * lnw_ref[...] + lnb_ref[...]
    bonus = _dot(rr * hk * rk_ref[...], evv.astype(F32), precision=HIGHEST) * rv
    o_ref[...] = ((y + bonus) * g).astype(o_ref.dtype)


def _rwkv_consts():
    c = RW_CHUNK
    n = N_HEADS * c
    r = np.arange(n)
    same = (r[:, None] // c) == (r[None, :] // c)
    sl = same & ((r[None, :] % c) < (r[:, None] % c))
    il = same & ((r[None, :] % c) <= (r[:, None] % c))
    tri = (np.arange(c)[None, :] <= np.arange(c)[:, None]).astype(np.float32)
    hv = _head_of(RW_W, HEAD_DIM)
    hm = (hv[None, :] == np.arange(N_HEADS)[:, None]).astype(np.float32)
    bd = (hv[:, None] == hv[None, :]).astype(np.float32)
    return (jnp.asarray(tri, F32), jnp.asarray(hm, F32), jnp.asarray(sl, F32), jnp.asarray(il, F32),
            jnp.asarray(np.eye(n), F32), jnp.asarray(bd, F32), jnp.asarray(bd, BF16))


def _rwkv_group(p_rw, params, batch, seq, tt):
    consts = _rwkv_consts()
    nt = seq // tt
    row_spec = lambda w: pl.BlockSpec((tt, w), lambda b, t: (b * nt + t, 0))
    ops = tuple(params) + consts
    return pl.pallas_call(
        functools.partial(_rwkv_kernel, tt=tt),
        grid=(batch, nt),
        in_specs=[row_spec(RW_COLS)] + [_const_spec(o.shape) for o in ops],
        out_specs=row_spec(RW_W),
        out_shape=jax.ShapeDtypeStruct((batch * seq, RW_W), BF16),
        scratch_shapes=[pltpu.VMEM((tt + 2 * SUBLANES, RW_COLS), F32), pltpu.VMEM((RW_W, RW_W), F32)],
        compiler_params=_cparams(2),
        name="rwkv_group",
    )(p_rw, *ops)


def _mem_kernel(q_ref, k_ref, v_ref, o_ref):
    q = (q_ref[...] * (HEAD_DIM ** -0.5)).astype(BF16)
    acc = jnp.zeros(o_ref.shape, F32)
    for h in range(N_HEADS):
        s = _dot_nt(q, k_ref[0, h])
        e = jnp.exp(s - jnp.max(s, axis=-1, keepdims=True))
        l = jnp.sum(e, axis=-1, keepdims=True)
        acc = acc + _dot(e.astype(BF16), v_ref[0, h]) / l
    o_ref[...] = acc.astype(o_ref.dtype)


def _mem_group(q_mem, k_exp, v_exp, batch, seq, tt):
    nt = seq // tt
    n_mem = k_exp.shape[2]
    kv_spec = pl.BlockSpec((1, N_HEADS, n_mem, MEM_W), lambda b, t: (b, 0, 0, 0))
    row_spec = pl.BlockSpec((tt, MEM_W), lambda b, t: (b * nt + t, 0))
    return pl.pallas_call(
        _mem_kernel,
        grid=(batch, nt),
        in_specs=[row_spec, kv_spec, kv_spec],
        out_specs=row_spec,
        out_shape=jax.ShapeDtypeStruct((batch * seq, MEM_W), BF16),
        compiler_params=_cparams(2),
        name="mem_attn",
    )(q_mem, k_exp, v_exp)


def _layer_norm(z, g, b):
    mu = jnp.mean(z, axis=-1, keepdims=True)
    zc = z - mu
    var = jnp.mean(zc * zc, axis=-1, keepdims=True)
    return zc * lax.rsqrt(var + LN_EPS) * g + b


def _outproj_kernel(oa_ref, ob_ref, oc_ref, od_ref, x_ref, w_ref, g_ref, b_ref, y_ref, *, alpha):
    h = _dot(oa_ref[...], w_ref[0:256, :])
    h = h + _dot(ob_ref[...], w_ref[256:512, :])
    h = h + _dot(oc_ref[...], w_ref[512:768, :])
    h = h + _dot(od_ref[...], w_ref[768:1024, :])
    y_ref[...] = _layer_norm(alpha * x_ref[...] + h, g_ref[...], b_ref[...])


def _out_project(groups, x2d, w_out, ln_g, ln_b, alpha, tm):
    m = x2d.shape[0]
    g_spec = pl.BlockSpec((tm, GROUP_W), lambda i: (i, 0))
    x_spec = pl.BlockSpec((tm, D_MODEL), lambda i: (i, 0))
    return pl.pallas_call(
        functools.partial(_outproj_kernel, alpha=alpha),
        grid=(m // tm,),
        in_specs=[g_spec] * 4 + [x_spec, _const_spec(w_out.shape), _const_spec(ln_g.shape), _const_spec(ln_b.shape)],
        out_specs=x_spec,
        out_shape=jax.ShapeDtypeStruct((m, D_MODEL), F32),
        compiler_params=_cparams(1),
        name="out_proj_ln",
    )(*groups, x2d, w_out, ln_g, ln_b)


def _router_kernel(x_ref, w_ref, b_ref, tri_ref, idx_ref, gate_ref, rank_ref, cnt_ref, carry_ref):
    i = pl.program_id(0)

    @pl.when(i == 0)
    def _():
        carry_ref[...] = jnp.zeros_like(carry_ref)

    tm = x_ref.shape[0]
    lane = lax.broadcasted_iota(jnp.int32, (tm, LANES), 1)
    logits = _dot(x_ref[...], w_ref[...], precision=HIGHEST) + b_ref[...]
    neg = jnp.float32(-3.0e38)
    work = jnp.where(lane < N_EXPERTS, logits, neg)
    vals, idxs = [], []
    sel = jnp.zeros((tm, LANES), F32)
    for _ in range(TOP_K):
        m = jnp.max(work, axis=-1, keepdims=True)
        idx = jnp.min(jnp.where(work == m, lane, LANES), axis=-1, keepdims=True)
        hit = lane == idx
        sel = jnp.where(hit, 1.0, sel)
        work = jnp.where(hit, neg, work)
        vals.append(m)
        idxs.append(idx)
    es = [jnp.exp(v - vals[0]) for v in vals]
    den = es[0] + es[1] + es[2] + es[3]
    incl = _dot(tri_ref[...], sel.astype(BF16))
    rank_dense = incl - sel + carry_ref[...]
    carry_ref[...] = carry_ref[...] + incl[tm - 1:tm, :]
    cnt_ref[...] = carry_ref[...]
    idx_out = jnp.zeros((tm, LANES), jnp.int32)
    gate_out = jnp.zeros((tm, LANES), F32)
    rank_out = jnp.zeros((tm, LANES), F32)
    for kk in range(TOP_K):
        rk = jnp.sum(jnp.where(lane == idxs[kk], rank_dense, 0.0), axis=-1, keepdims=True)
        here = lane == kk
        idx_out = jnp.where(here, idxs[kk], idx_out)
        gate_out = jnp.where(here, es[kk] / den, gate_out)
        rank_out = jnp.where(here, rk, rank_out)
    idx_ref[...] = idx_out
    gate_ref[...] = gate_out
    rank_ref[...] = rank_out


def _route(x2d, rw_pad, rb_pad, tm):
    m = x2d.shape[0]
    tri = jnp.asarray(np.tril(np.ones((tm, tm), np.float32)), BF16)
    row = pl.BlockSpec((tm, LANES), lambda i: (i, 0))
    return pl.pallas_call(
        _router_kernel,
        grid=(m // tm,),
        in_specs=[pl.BlockSpec((tm, D_MODEL), lambda i: (i, 0)), _const_spec(rw_pad.shape),
                  _const_spec(rb_pad.shape), _const_spec(tri.shape)],
        out_specs=[row, row, row, _const_spec((1, LANES))],
        out_shape=[jax.ShapeDtypeStruct((m, LANES), jnp.int32), jax.ShapeDtypeStruct((m, LANES), F32),
                   jax.ShapeDtypeStruct((m, LANES), F32), jax.ShapeDtypeStruct((1, LANES), F32)],
        scratch_shapes=[pltpu.VMEM((1, LANES), F32)],
        compiler_params=_cparams(1),
        name="router",
    )(x2d, rw_pad, rb_pad, tri)


def _expert_kernel(be_ref, nu_ref, x_ref, wg_ref, wu_ref, bg_ref, bu_ref, wd_ref, bd_ref, y_ref):
    i = pl.program_id(0)

    @pl.when(i < nu_ref[0])
    def _():
        x = x_ref[...]
        hg = _dot(x, wg_ref[0]) + bg_ref[0]
        hu = _dot(x, wu_ref[0]) + bu_ref[0]
        gate = jnp.minimum(hg, SWIGLU_LIMIT)
        up = jnp.clip(hu, -SWIGLU_LIMIT, SWIGLU_LIMIT)
        glu = gate * _sigmoid(gate * SWIGLU_ALPHA)
        act = ((up + 1.0) * glu).astype(BF16)
        y_ref[...] = (_dot(act, wd_ref[0]) + bd_ref[0]).astype(y_ref.dtype)

    @pl.when(i >= nu_ref[0])
    def _():
        y_ref[...] = jnp.zeros_like(y_ref)


def _expert_ffn(xs, block_e, n_used, w_g, w_u, b_g, b_u, w_d, b_d, tm):
    p = xs.shape[0]
    nb = p // tm
    f = w_g.shape[2]
    w_spec = lambda s: pl.BlockSpec((1,) + s, lambda i, be, nu: (be[i], 0, 0))
    grid_spec = pltpu.PrefetchScalarGridSpec(
        num_scalar_prefetch=2,
        grid=(nb,),
        in_specs=[pl.BlockSpec((tm, D_MODEL), lambda i, be, nu: (i, 0)),
                  w_spec((D_MODEL, f)), w_spec((D_MODEL, f)), w_spec((1, f)), w_spec((1, f)),
                  w_spec((f, D_MODEL)), w_spec((1, D_MODEL))],
        out_specs=pl.BlockSpec((tm, D_MODEL), lambda i, be, nu: (i, 0)),
    )
    return pl.pallas_call(
        _expert_kernel,
        grid_spec=grid_spec,
        out_shape=jax.ShapeDtypeStruct((p, D_MODEL), F32),
        compiler_params=_cparams(1),
        name="expert_ffn",
    )(block_e, n_used, xs, w_g, w_u, b_g, b_u, w_d, b_d)


def _combine_kernel(yg_ref, gate_ref, x_ref, g_ref, b_ref, o_ref, *, alpha):
    tm = x_ref.shape[0]
    lane = lax.broadcasted_iota(jnp.int32, (tm, LANES), 1)
    gd = gate_ref[...]
    m = jnp.zeros((tm, D_MODEL), F32)
    for kk in range(TOP_K):
        gk = jnp.sum(jnp.where(lane == kk, gd, 0.0), axis=-1, keepdims=True)
        m = m + gk * yg_ref[kk]
    o_ref[...] = _layer_norm(alpha * x_ref[...] + m, g_ref[...], b_ref[...])


def _combine(yg, gates_dense, x2d, ln_g, ln_b, alpha, tm):
    m = x2d.shape[0]
    x_spec = pl.BlockSpec((tm, D_MODEL), lambda i: (i, 0))
    return pl.pallas_call(
        functools.partial(_combine_kernel, alpha=alpha),
        grid=(m // tm,),
        in_specs=[pl.BlockSpec((TOP_K, tm, D_MODEL), lambda i: (0, i, 0)),
                  pl.BlockSpec((tm, LANES), lambda i: (i, 0)), x_spec,
                  _const_spec(ln_g.shape), _const_spec(ln_b.shape)],
        out_specs=x_spec,
        out_shape=jax.ShapeDtypeStruct((m, D_MODEL), F32),
        compiler_params=_cparams(1),
        name="combine_ln",
    )(yg, gates_dense, x2d, ln_g, ln_b)


def _tile(n, pref):
    t = min(n, pref)
    assert n % t == 0
    return t


def _pad_rows(w, rows, at):
    out = jnp.zeros((rows, w.shape[1]), w.dtype)
    return out.at[at:at + w.shape[0]].set(w)


def _moe(x1, router_w, router_b, w_g, w_u, b_g, b_u, w_d, b_d, ln_g, ln_b, alpha, tm_tok, tm_exp):
    n_tok = x1.shape[0]
    rw_pad = jnp.zeros((D_MODEL, LANES), F32).at[:, :N_EXPERTS].set(router_w)
    rb_pad = jnp.zeros((1, LANES), F32).at[0, :N_EXPERTS].set(router_b)
    idx_d, gate_d, rank_d, counts_d = _route(x1, rw_pad, rb_pad, tm_tok)
    top_idx = idx_d[:, :TOP_K]
    rank = rank_d[:, :TOP_K].astype(jnp.int32)
    counts = counts_d[0, :N_EXPERTS].astype(jnp.int32)
    padded = (counts + tm_exp - 1) // tm_exp * tm_exp
    pad_end = jnp.cumsum(padded)
    pad_start = pad_end - padded
    dest = pad_start[top_idx] + rank
    n_blocks = -(-(n_tok * TOP_K) // tm_exp) + N_EXPERTS
    n_slots = n_blocks * tm_exp
    n_used = (pad_end[-1] // tm_exp).astype(jnp.int32)
    blk = jnp.arange(n_blocks, dtype=jnp.int32)
    block_e = jnp.searchsorted(pad_end, jnp.minimum(blk, n_used - 1) * tm_exp, side='right').astype(jnp.int32)
    block_e = jnp.minimum(block_e, N_EXPERTS - 1)
    tok = jnp.broadcast_to(jnp.arange(n_tok, dtype=jnp.int32)[:, None], (n_tok, TOP_K))
    slot_tok = jnp.full((n_slots,), n_tok, jnp.int32).at[dest.reshape(-1)].set(tok.reshape(-1))
    x_pad = jnp.concatenate([x1.astype(BF16), jnp.zeros((1, D_MODEL), BF16)], axis=0)
    xs = x_pad[slot_tok]
    yb = _expert_ffn(xs, block_e, n_used.reshape(1), w_g, w_u, b_g, b_u, w_d, b_d, tm_exp)
    yg = yb[dest.T]
    return _combine(yg, gate_d, x1, ln_g, ln_b, alpha, tm_tok)


def kernel(x, mem, w_in, gla_conv, gla_gate_w2, gla_gate_b, gla_norm_g, hg_lower_bound, hg_norm_g, rw_mu, rw_w0, rw_w2, rw_a0, rw_a2, rw_g2, rw_k_k, rw_k_a, rw_r_k, rw_ln_w, rw_ln_b, w_mem_k, w_mem_v, w_out, ln1_g, ln1_b, router_w, router_b, w_gu, b_gu, w_down, b_down, ln2_g, ln2_b):
    batch, seq, d = x.shape
    depth = w_in.shape[0]
    n_mem = mem.shape[1]
    n_tok = batch * seq
    alpha = float((2.0 * depth) ** 0.25)
    tt = _tile(seq, 256)
    tm = _tile(n_tok, 512)
    tm_exp = 512

    lb = jax.nn.softmax(hg_lower_bound.astype(F32), axis=0)
    lb = jnp.cumsum(lb, axis=0) - lb[0]

    hv = _head_of(MEM_W, HEAD_DIM)
    head_mask = jnp.asarray((hv[None, :] == np.arange(N_HEADS)[:, None]).astype(np.float32))

    c0 = GLA_COLS
    c1 = c0 + HG_COLS
    c2 = c1 + RW_COLS
    widths = (GLA_PAD, HG_COLS, RW_COLS, MEM_W)
    xc = x.reshape(n_tok, d)
    mem2d = mem.reshape(batch * n_mem, d)
    row = lambda v: v.reshape(1, -1).astype(F32)
    for l in range(depth):
        wl = w_in[l]
        w_gla = jnp.concatenate([wl[:, 0:GLA_CONV_C], wl[:, GLA_CONV_C + GLA_GATE_RANK:c0],
                                 wl[:, GLA_CONV_C:GLA_CONV_C + GLA_GATE_RANK],
                                 jnp.zeros((d, GLA_PAD - GLA_COLS), F32)], axis=1)
        w_all = jnp.concatenate([w_gla, wl[:, c0:]], axis=1).astype(BF16)
        p_gla, p_hg, p_rw, q_mem = _project(xc, w_all, widths, tm)

        w2pad = _pad_rows(gla_gate_w2[l], GLA_PAD - GLA_CONV_C - GLA_V, 0)
        o_gla = _gla_group(p_gla, gla_conv[l], w2pad, row(gla_gate_b[l]), row(gla_norm_g[l]), batch, seq, tt)
        o_hg = _hgrn_group(p_hg, row(lb[l]), row(hg_norm_g[l]), batch, seq, tt)
        rw_params = (row(rw_mu[l]), row(rw_w0[l]), _pad_rows(rw_w2[l], LANES, 0), row(rw_a0[l]),
                     _pad_rows(rw_a2[l], LANES, RW_DECAY_RANK), rw_g2[l].astype(BF16), row(rw_k_k[l]),
                     row(rw_k_a[l]), row(rw_r_k[l]), row(rw_ln_w[l]), row(rw_ln_b[l]))
        o_rw = _rwkv_group(p_rw, rw_params, batch, seq, tt)

        w_kv = jnp.concatenate([w_mem_k[l], w_mem_v[l]], axis=1).astype(BF16)
        mk, mv = _project(mem2d, w_kv, (MEM_W, MEM_W), _tile(batch * n_mem, 512))
        k_exp = (mk.reshape(batch, 1, n_mem, MEM_W) * head_mask[None, :, None, :]).astype(BF16)
        v_exp = (mv.reshape(batch, 1, n_mem, MEM_W) * head_mask[None, :, None, :]).astype(BF16)
        o_mem = _mem_group(q_mem, k_exp, v_exp, batch, seq, _tile(seq, 512))

        x1 = _out_project((o_gla, o_hg, o_rw, o_mem), xc, w_out[l].astype(BF16), row(ln1_g[l]), row(ln1_b[l]), alpha, tm)

        w_g = w_gu[l][:, :, 0::2].astype(BF16)
        w_u = w_gu[l][:, :, 1::2].astype(BF16)
        b_g = b_gu[l][:, None, 0::2]
        b_u = b_gu[l][:, None, 1::2]
        xc = _moe(x1, router_w[l], router_b[l], w_g, w_u, b_g, b_u, w_down[l].astype(BF16), b_down[l][:, None, :],
                  row(ln2_g[l]), row(ln2_b[l]), alpha, tm, tm_exp)
    return xc.reshape(batch, seq, d)
```

```python
import functools

import numpy as np
import jax
import jax.numpy as jnp
from jax import lax
from jax.experimental import pallas as pl
from jax.experimental.pallas import tpu as pltpu

F32 = jnp.float32
BF16 = jnp.bfloat16
HIGHEST = lax.Precision.HIGHEST

D_MODEL = 1024
HEAD_DIM = 64
N_HEADS = 4
GROUP_W = 256
GLA_DK = 32
GLA_QK = 128
GLA_V = 256
GLA_GATE_RANK = 16
GLA_TAU = 16.0
CONV_W = 4
GLA_CONV_C = 2 * GLA_QK + GLA_V
GLA_COLS = 784
GLA_PAD = 896
HG_COLS = 1024
RW_W = 256
RW_DECAY_RANK = 64
RW_A_RANK = 64
RW_GATE_RANK = 128
RW_COLS = 1024
RW_LN_EPS = 64e-5
MEM_W = 256
N_EXPERTS = 32
TOP_K = 4
SWIGLU_ALPHA = 1.702
SWIGLU_LIMIT = 7.0
LN_EPS = 1e-5
RMS_EPS = 1e-6
GATE_FLOOR = 1e-30
LOG2E = 1.4426950408889634
NEG_BIG = -1.0e30

LANES = 128
SUBLANES = 8
SUB_CHUNK = 16
RW_CHUNK = 64
VMEM_LIMIT = 48 * 1024 * 1024


def _cparams(n_axes):
    return pltpu.CompilerParams(dimension_semantics=("arbitrary",) * n_axes, vmem_limit_bytes=VMEM_LIMIT)


def _dot(a, b, precision=None):
    return jnp.dot(a, b, preferred_element_type=F32, precision=precision)


def _dot_nt(a, b):
    return lax.dot_general(a, b, (((1,), (1,)), ((), ())), preferred_element_type=F32)


def _dot_tn(a, b):
    return lax.dot_general(a, b, (((0,), (0,)), ((), ())), preferred_element_type=F32)


def _split_bf16(x, terms):
    parts = []
    for _ in range(terms):
        p = x.astype(BF16)
        parts.append(p)
        x = x - p.astype(F32)
    return parts


def _dot_onehot_rhs(x, e_bf16, terms):
    parts = _split_bf16(x, terms)
    acc = _dot(parts[0], e_bf16)
    for p in parts[1:]:
        acc = acc + _dot(p, e_bf16)
    return acc


def _dot_onehot_lhs(e_bf16, x, terms):
    parts = _split_bf16(x, terms)
    acc = _dot(e_bf16, parts[0])
    for p in parts[1:]:
        acc = acc + _dot(e_bf16, p)
    return acc


def _sigmoid(x):
    return 1.0 / (1.0 + jnp.exp(-x))


def _silu(x):
    return x * _sigmoid(x)


def _softplus(x):
    return jnp.maximum(x, 0.0) + jnp.log(1.0 + jnp.exp(-jnp.abs(x)))


def _const_spec(shape):
    nd = len(shape)
    return pl.BlockSpec(shape, lambda *_: (0,) * nd)


def _proj_kernel(x_ref, w_ref, *out_refs, widths):
    xb = x_ref[...].astype(BF16)
    off = 0
    for o_ref, wd in zip(out_refs, widths):
        o_ref[...] = _dot(xb, w_ref[:, off:off + wd]).astype(o_ref.dtype)
        off += wd


def _project(x2d, w_bf16, widths, tm):
    m, k = x2d.shape
    n = w_bf16.shape[1]
    assert sum(widths) == n and m % tm == 0
    return pl.pallas_call(
        functools.partial(_proj_kernel, widths=tuple(widths)),
        grid=(m // tm,),
        in_specs=[pl.BlockSpec((tm, k), lambda i: (i, 0)), _const_spec((k, n))],
        out_specs=[pl.BlockSpec((tm, wd), lambda i: (i, 0)) for wd in widths],
        out_shape=[jax.ShapeDtypeStruct((m, wd), F32) for wd in widths],
        compiler_params=_cparams(1),
        name="in_proj",
    )(x2d, w_bf16)


def _gla_core(q, k, v, g, lt_ref, ekv_ref, bdt_ref, st_ref):
    tt, ck = q.shape
    cv = v.shape[1]
    nc = tt // SUB_CHUNK
    b = _dot_onehot_lhs(lt_ref[...], g * LOG2E, 3)
    q3 = q.reshape(nc, SUB_CHUNK, ck)
    k3 = k.reshape(nc, SUB_CHUNK, ck)
    v3 = v.reshape(nc, SUB_CHUNK, cv)
    b3 = b.reshape(nc, SUB_CHUNK, ck)
    row = lax.broadcasted_iota(jnp.int32, (nc, SUB_CHUNK, ck), 1)
    ekv = ekv_ref[...]

    o3 = jnp.zeros((nc, SUB_CHUNK, cv), F32)
    for j in range(SUB_CHUNK):
        rel = jnp.where(row >= j, b3 - b3[:, j:j + 1, :], NEG_BIG)
        w = q3 * jnp.exp2(rel) * k3[:, j:j + 1, :]
        s = _dot(w.reshape(tt, ck).astype(BF16), ekv).reshape(nc, SUB_CHUNK, cv)
        o3 = o3 + s * v3[:, j:j + 1, :]

    bend3 = b3[:, SUB_CHUNK - 1:SUB_CHUNK, :]
    qg = (q3 * jnp.exp2(b3)).reshape(tt, ck).astype(BF16)
    kg = (k3 * jnp.exp2(bend3 - b3)).reshape(tt, ck).astype(BF16)
    dec = jnp.exp2(bend3)
    vb = v.astype(BF16)
    bdt = bdt_ref[...]
    st = st_ref[...]
    outs = []
    for c in range(nc):
        sl = slice(c * SUB_CHUNK, (c + 1) * SUB_CHUNK)
        outs.append(_dot_nt(qg[sl], st.astype(BF16)))
        st = st * dec[c] + bdt * _dot_tn(vb[sl], kg[sl])
    st_ref[...] = st
    return o3.reshape(tt, cv) + jnp.concatenate(outs, axis=0)


def _head_rms_gate(o, evv, norm_g, gate):
    ms = _dot((o * o).astype(BF16), evv) * (1.0 / HEAD_DIM)
    return o * lax.rsqrt(ms + RMS_EPS) * norm_g * _silu(gate)


def _gla_kernel(p_ref, conv_ref, w2_ref, gb_ref, ng_ref, lt_ref, ekv_ref, bdt_ref, evv_ref,
                o_ref, xp_ref, st_ref, *, tt):
    t = pl.program_id(1)

    @pl.when(t == 0)
    def _():
        st_ref[...] = jnp.zeros_like(st_ref)
        xp_ref[tt:tt + SUBLANES, :] = jnp.zeros((SUBLANES, GLA_CONV_C), F32)

    xp_ref[0:SUBLANES, :] = xp_ref[tt:tt + SUBLANES, :]
    xp_ref[SUBLANES:tt + SUBLANES, :] = p_ref[:, 0:GLA_CONV_C]
    acc = jnp.zeros((tt, GLA_CONV_C), F32)
    for i in range(CONV_W):
        shift = CONV_W - 1 - i
        acc = acc + xp_ref[SUBLANES - shift:SUBLANES - shift + tt, :] * conv_ref[i:i + 1, :]
    qkv = _silu(acc)
    q = qkv[:, 0:GLA_QK] * (GLA_DK ** -0.5)
    k = qkv[:, GLA_QK:2 * GLA_QK]
    v = qkv[:, 2 * GLA_QK:GLA_CONV_C]
    g_out = p_ref[:, GLA_CONV_C:GLA_CONV_C + GLA_V]
    z = _dot(p_ref[:, GLA_CONV_C + GLA_V:GLA_PAD], w2_ref[...], precision=HIGHEST) + gb_ref[...]
    log_alpha = -_softplus(-z) * (1.0 / GLA_TAU)
    o = _gla_core(q, k, v, log_alpha, lt_ref, ekv_ref, bdt_ref, st_ref)
    o_ref[...] = _head_rms_gate(o, evv_ref[...], ng_ref[...], g_out).astype(o_ref.dtype)


def _hgrn_kernel(p_ref, lb_ref, ng_ref, lt_ref, ekv_ref, bdt_ref, evv_ref, o_ref, st_ref):
    t = pl.program_id(1)

    @pl.when(t == 0)
    def _():
        st_ref[...] = jnp.zeros_like(st_ref)

    hq = p_ref[:, 0:256]
    hf = p_ref[:, 256:512]
    hi = p_ref[:, 512:768]
    hgate = p_ref[:, 768:1024]
    lb = lb_ref[...]
    f_gate = lb + (1.0 - lb) * _sigmoid(hf)
    log_f = jnp.log(jnp.maximum(f_gate, GATE_FLOOR))
    k_in = (1.0 - lb) * _sigmoid(-hf)
    o = _gla_core(_silu(hq), k_in, hi, log_f, lt_ref, ekv_ref, bdt_ref, st_ref)
    o_ref[...] = _head_rms_gate(o, evv_ref[...], ng_ref[...], hgate).astype(o_ref.dtype)


def _head_of(n, width):
    return np.arange(n) // width


def _gla_consts(tt, ck, cv):
    r = np.arange(tt)
    lt = ((r[:, None] // SUB_CHUNK == r[None, :] // SUB_CHUNK) & (r[None, :] <= r[:, None])).astype(np.float32)
    hk = _head_of(ck, ck // N_HEADS)
    hv = _head_of(cv, cv // N_HEADS)
    ekv = (hk[:, None] == hv[None, :]).astype(np.float32)
    evv = (hv[:, None] == hv[None, :]).astype(np.float32)
    return (jnp.asarray(lt, BF16), jnp.asarray(ekv, BF16), jnp.asarray(ekv.T, F32), jnp.asarray(evv, BF16))


def _gla_group(p_gla, conv_w, w2pad, gate_b, norm_g, batch, seq, tt):
    lt, ekv, bdt, evv = _gla_consts(tt, GLA_QK, GLA_V)
    nt = seq // tt
    row_spec = lambda w: pl.BlockSpec((tt, w), lambda b, t: (b * nt + t, 0))
    return pl.pallas_call(
        functools.partial(_gla_kernel, tt=tt),
        grid=(batch, nt),
        in_specs=[row_spec(GLA_PAD), _const_spec(conv_w.shape), _const_spec(w2pad.shape),
                  _const_spec(gate_b.shape), _const_spec(norm_g.shape), _const_spec(lt.shape),
                  _const_spec(ekv.shape), _const_spec(bdt.shape), _const_spec(evv.shape)],
        out_specs=row_spec(GLA_V),
        out_shape=jax.ShapeDtypeStruct((batch * seq, GLA_V), BF16),
        scratch_shapes=[pltpu.VMEM((tt + 2 * SUBLANES, GLA_CONV_C), F32), pltpu.VMEM((GLA_V, GLA_QK), F32)],
        compiler_params=_cparams(2),
        name="gla_group",
    )(p_gla, conv_w, w2pad, gate_b, norm_g, lt, ekv, bdt, evv)


def _hgrn_group(p_hg, lb, norm_g, batch, seq, tt):
    lt, ekv, bdt, evv = _gla_consts(tt, 256, 256)
    nt = seq // tt
    row_spec = lambda w: pl.BlockSpec((tt, w), lambda b, t: (b * nt + t, 0))
    return pl.pallas_call(
        _hgrn_kernel,
        grid=(batch, nt),
        in_specs=[row_spec(HG_COLS), _const_spec(lb.shape), _const_spec(norm_g.shape), _const_spec(lt.shape),
                  _const_spec(ekv.shape), _const_spec(bdt.shape), _const_spec(evv.shape)],
        out_specs=row_spec(256),
        out_shape=jax.ShapeDtypeStruct((batch * seq, 256), BF16),
        scratch_shapes=[pltpu.VMEM((256, 256), F32)],
        compiler_params=_cparams(2),
        name="hgrn_group",
    )(p_hg, lb, norm_g, lt, ekv, bdt, evv)


def _expand_heads(x, hm):
    return jnp.concatenate([x * hm[h:h + 1, :] for h in range(N_HEADS)], axis=0)


def _fold_heads(x_e, c):
    out = x_e[0:c]
    for h in range(1, N_HEADS):
        out = out + x_e[h * c:(h + 1) * c]
    return out


def _rwkv_kernel(p_ref, mu_ref, w0_ref, w2_ref, a0_ref, a2_ref, g2_ref, kk_ref, ka_ref, rk_ref,
                 lnw_ref, lnb_ref, tri_ref, hm_ref, sl_ref, il_ref, eye_ref, bd_ref, evv_ref,
                 o_ref, xp_ref, st_ref, *, tt):
    t = pl.program_id(1)

    @pl.when(t == 0)
    def _():
        st_ref[...] = jnp.zeros_like(st_ref)
        xp_ref[tt:tt + SUBLANES, :] = jnp.zeros((SUBLANES, RW_COLS), F32)

    xp_ref[0:SUBLANES, :] = xp_ref[tt:tt + SUBLANES, :]
    xp_ref[SUBLANES:tt + SUBLANES, :] = p_ref[...]
    cur = p_ref[...]
    prev = xp_ref[SUBLANES - 1:SUBLANES - 1 + tt, :]
    p = cur + (prev - cur) * mu_ref[...]
    rr = p[:, 0:256]
    rk = p[:, 256:512]
    rv = p[:, 512:768]
    lr = p[:, 768:896]
    g_lr = p[:, 896:1024]
    w_log = -_softplus(-(w0_ref[...] + _dot(jnp.tanh(lr), w2_ref[...], precision=HIGHEST))) - 0.5
    lw = -jnp.exp(w_log)
    a = _sigmoid(a0_ref[...] + _dot(lr, a2_ref[...], precision=HIGHEST))
    g = _dot(_sigmoid(g_lr).astype(BF16), g2_ref[...])
    evv = evv_ref[...]
    kk = rk * kk_ref[...]
    kk_sq = _dot_onehot_rhs(kk * kk, evv, 2)
    kk = kk / jnp.maximum(jnp.sqrt(kk_sq), 1e-12)
    hk = rk * (1.0 + (a - 1.0) * ka_ref[...])
    a_vec = -kk
    b_vec = kk * a

    hm = hm_ref[...]
    sl_m = sl_ref[...]
    il_m = il_ref[...]
    bd = bd_ref[...]
    eye = eye_ref[...]
    tri = tri_ref[...]
    c = RW_CHUNK
    st = st_ref[...]
    ys = []
    for ci in range(tt // c):
        s_ = slice(ci * c, (ci + 1) * c)
        lw_c = lw[s_]
        cs = _dot_onehot_lhs(tri, lw_c, 3)
        gam = jnp.exp(cs)
        igam = jnp.exp(-cs)
        gam_end = jnp.exp(cs[c - 1:c, :])
        at = a_vec[s_] * jnp.exp(cs - lw_c)
        bh = b_vec[s_] * igam
        kh = hk[s_] * igam
        rt = rr[s_] * gam
        v_c = rv[s_]
        at_e = _expand_heads(at, hm).astype(BF16)
        bh_e = _expand_heads(bh, hm).astype(BF16)
        kh_e = _expand_heads(kh, hm).astype(BF16)
        rt_e = _expand_heads(rt, hm).astype(BF16)
        v_e = _expand_heads(v_c, hm).astype(BF16)
        l_ab = _dot_nt(at_e, bh_e) * sl_m
        l_ak = _dot_nt(at_e, kh_e) * sl_m
        l_rb = _dot_nt(rt_e, bh_e) * il_m
        l_rk = _dot_nt(rt_e, kh_e) * il_m
        tm = eye + l_ab
        pw = l_ab
        for _ in range(int(np.log2(c)) - 1):
            pwb = pw.astype(BF16)
            pw = _dot(pwb, pwb)
            tm = tm + _dot(tm.astype(BF16), pw.astype(BF16))
        tmb = tm.astype(BF16)
        w_e = _dot(tmb, at_e)
        uv_e = _dot(tmb, _dot(l_ak.astype(BF16), v_e).astype(BF16))
        stb = st.astype(BF16)
        u_e = _dot_nt(w_e.astype(BF16), stb) + uv_e
        y_e = _dot_nt(rt_e, stb) + _dot(l_rb.astype(BF16), u_e.astype(BF16)) + _dot(l_rk.astype(BF16), v_e)
        ys.append(_fold_heads(y_e, c))
        u = _fold_heads(u_e, c)
        lhs = jnp.concatenate([u, v_c], axis=0).astype(BF16)
        rhs = jnp.concatenate([bh * gam_end, kh * gam_end], axis=0).astype(BF16)
        st = st * gam_end + bd * _dot_tn(lhs, rhs)
    st_ref[...] = st
    y = jnp.concatenate(ys, axis=0)

    inv_n = 1.0 / HEAD_DIM
    mu = _dot(y.astype(BF16), evv) * inv_n
    yc = y - mu
    var = _dot((yc * yc).astype(BF16), evv) * inv_n
    y = yc * lax.rsqrt(var + RW_LN_EPS) * lnw_ref[...] + lnb_ref[...]
    bonus = _dot_onehot_rhs(rr * hk * rk_ref[...], evv, 2) * rv
    o_ref[...] = ((y + bonus) * g).astype(o_ref.dtype)


def _rwkv_consts():
    c = RW_CHUNK
    n = N_HEADS * c
    r = np.arange(n)
    same = (r[:, None] // c) == (r[None, :] // c)
    sl = same & ((r[None, :] % c) < (r[:, None] % c))
    il = same & ((r[None, :] % c) <= (r[:, None] % c))
    tri = (np.arange(c)[None, :] <= np.arange(c)[:, None]).astype(np.float32)
    hv = _head_of(RW_W, HEAD_DIM)
    hm = (hv[None, :] == np.arange(N_HEADS)[:, None]).astype(np.float32)
    bd = (hv[:, None] == hv[None, :]).astype(np.float32)
    return (jnp.asarray(tri, BF16), jnp.asarray(hm, F32), jnp.asarray(sl, F32), jnp.asarray(il, F32),
            jnp.asarray(np.eye(n), F32), jnp.asarray(bd, F32), jnp.asarray(bd, BF16))


def _rwkv_group(p_rw, params, batch, seq, tt):
    consts = _rwkv_consts()
    nt = seq // tt
    row_spec = lambda w: pl.BlockSpec((tt, w), lambda b, t: (b * nt + t, 0))
    ops = tuple(params) + consts
    return pl.pallas_call(
        functools.partial(_rwkv_kernel, tt=tt),
        grid=(batch, nt),
        in_specs=[row_spec(RW_COLS)] + [_const_spec(o.shape) for o in ops],
        out_specs=row_spec(RW_W),
        out_shape=jax.ShapeDtypeStruct((batch * seq, RW_W), BF16),
        scratch_shapes=[pltpu.VMEM((tt + 2 * SUBLANES, RW_COLS), F32), pltpu.VMEM((RW_W, RW_W), F32)],
        compiler_params=_cparams(2),
        name="rwkv_group",
    )(p_rw, *ops)


def _mem_kernel(q_ref, k_ref, v_ref, o_ref):
    q = (q_ref[...] * (HEAD_DIM ** -0.5)).astype(BF16)
    acc = jnp.zeros(o_ref.shape, F32)
    for h in range(N_HEADS):
        s = _dot_nt(q, k_ref[0, h])
        e = jnp.exp(s - jnp.max(s, axis=-1, keepdims=True))
        l = jnp.sum(e, axis=-1, keepdims=True)
        acc = acc + _dot(e.astype(BF16), v_ref[0, h]) / l
    o_ref[...] = acc.astype(o_ref.dtype)


def _mem_group(q_mem, k_exp, v_exp, batch, seq, tt):
    nt = seq // tt
    n_mem = k_exp.shape[2]
    kv_spec = pl.BlockSpec((1, N_HEADS, n_mem, MEM_W), lambda b, t: (b, 0, 0, 0))
    row_spec = pl.BlockSpec((tt, MEM_W), lambda b, t: (b * nt + t, 0))
    return pl.pallas_call(
        _mem_kernel,
        grid=(batch, nt),
        in_specs=[row_spec, kv_spec, kv_spec],
        out_specs=row_spec,
        out_shape=jax.ShapeDtypeStruct((batch * seq, MEM_W), BF16),
        compiler_params=_cparams(2),
        name="mem_attn",
    )(q_mem, k_exp, v_exp)


def _layer_norm(z, g, b):
    mu = jnp.mean(z, axis=-1, keepdims=True)
    zc = z - mu
    var = jnp.mean(zc * zc, axis=-1, keepdims=True)
    return zc * lax.rsqrt(var + LN_EPS) * g + b


def _outproj_kernel(oa_ref, ob_ref, oc_ref, od_ref, x_ref, w_ref, g_ref, b_ref, y_ref, *, alpha):
    h = _dot(oa_ref[...], w_ref[0:256, :])
    h = h + _dot(ob_ref[...], w_ref[256:512, :])
    h = h + _dot(oc_ref[...], w_ref[512:768, :])
    h = h + _dot(od_ref[...], w_ref[768:1024, :])
    y_ref[...] = _layer_norm(alpha * x_ref[...] + h, g_ref[...], b_ref[...])


def _out_project(groups, x2d, w_out, ln_g, ln_b, alpha, tm):
    m = x2d.shape[0]
    g_spec = pl.BlockSpec((tm, GROUP_W), lambda i: (i, 0))
    x_spec = pl.BlockSpec((tm, D_MODEL), lambda i: (i, 0))
    return pl.pallas_call(
        functools.partial(_outproj_kernel, alpha=alpha),
        grid=(m // tm,),
        in_specs=[g_spec] * 4 + [x_spec, _const_spec(w_out.shape), _const_spec(ln_g.shape), _const_spec(ln_b.shape)],
        out_specs=x_spec,
        out_shape=jax.ShapeDtypeStruct((m, D_MODEL), F32),
        compiler_params=_cparams(1),
        name="out_proj_ln",
    )(*groups, x2d, w_out, ln_g, ln_b)


def _router_kernel(x_ref, w_ref, b_ref, tri_ref, idx_ref, gate_ref, rank_ref, cnt_ref, carry_ref):
    i = pl.program_id(0)

    @pl.when(i == 0)
    def _():
        carry_ref[...] = jnp.zeros_like(carry_ref)

    tm = x_ref.shape[0]
    lane = lax.broadcasted_iota(jnp.int32, (tm, LANES), 1)
    logits = _dot(x_ref[...], w_ref[...], precision=HIGHEST) + b_ref[...]
    neg = jnp.float32(-3.0e38)
    work = jnp.where(lane < N_EXPERTS, logits, neg)
    vals, idxs = [], []
    sel = jnp.zeros((tm, LANES), F32)
    for _ in range(TOP_K):
        m = jnp.max(work, axis=-1, keepdims=True)
        idx = jnp.min(jnp.where(work == m, lane, LANES), axis=-1, keepdims=True)
        hit = lane == idx
        sel = jnp.where(hit, 1.0, sel)
        work = jnp.where(hit, neg, work)
        vals.append(m)
        idxs.append(idx)
    es = [jnp.exp(v - vals[0]) for v in vals]
    den = es[0] + es[1] + es[2] + es[3]
    incl = _dot(tri_ref[...], sel.astype(BF16))
    rank_dense = incl - sel + carry_ref[...]
    carry_ref[...] = carry_ref[...] + incl[tm - 1:tm, :]
    cnt_ref[...] = carry_ref[...]
    idx_out = jnp.zeros((tm, LANES), jnp.int32)
    gate_out = jnp.zeros((tm, LANES), F32)
    rank_out = jnp.zeros((tm, LANES), F32)
    for kk in range(TOP_K):
        rk = jnp.sum(jnp.where(lane == idxs[kk], rank_dense, 0.0), axis=-1, keepdims=True)
        here = lane == kk
        idx_out = jnp.where(here, idxs[kk], idx_out)
        gate_out = jnp.where(here, es[kk] / den, gate_out)
        rank_out = jnp.where(here, rk, rank_out)
    idx_ref[...] = idx_out
    gate_ref[...] = gate_out
    rank_ref[...] = rank_out


def _route(x2d, rw_pad, rb_pad, tm):
    m = x2d.shape[0]
    tri = jnp.asarray(np.tril(np.ones((tm, tm), np.float32)), BF16)
    row = pl.BlockSpec((tm, LANES), lambda i: (i, 0))
    return pl.pallas_call(
        _router_kernel,
        grid=(m // tm,),
        in_specs=[pl.BlockSpec((tm, D_MODEL), lambda i: (i, 0)), _const_spec(rw_pad.shape),
                  _const_spec(rb_pad.shape), _const_spec(tri.shape)],
        out_specs=[row, row, row, _const_spec((1, LANES))],
        out_shape=[jax.ShapeDtypeStruct((m, LANES), jnp.int32), jax.ShapeDtypeStruct((m, LANES), F32),
                   jax.ShapeDtypeStruct((m, LANES), F32), jax.ShapeDtypeStruct((1, LANES), F32)],
        scratch_shapes=[pltpu.VMEM((1, LANES), F32)],
        compiler_params=_cparams(1),
        name="router",
    )(x2d, rw_pad, rb_pad, tri)


DEINT_W = 2 * LANES


def _deinterleave_kernel(w_ref, perm_ref, wg_ref, wu_ref):
    perm = perm_ref[...]
    for blk in range(w_ref.shape[2] // DEINT_W):
        wb = w_ref[0, :, blk * DEINT_W:(blk + 1) * DEINT_W].astype(BF16)
        r = _dot(wb, perm)
        wg_ref[0, :, blk * LANES:(blk + 1) * LANES] = r[:, 0:LANES].astype(BF16)
        wu_ref[0, :, blk * LANES:(blk + 1) * LANES] = r[:, LANES:DEINT_W].astype(BF16)


def _deinterleave_gate_up(w_gu, tk):
    n, k, f2 = w_gu.shape
    f = f2 // 2
    j = np.arange(DEINT_W)
    src = np.where(j < LANES, 2 * j, 2 * (j - LANES) + 1)
    perm = np.zeros((DEINT_W, DEINT_W), np.float32)
    perm[src, j] = 1.0
    perm = jnp.asarray(perm, BF16)
    out_spec = pl.BlockSpec((1, tk, f), lambda e, i: (e, i, 0))
    return pl.pallas_call(
        _deinterleave_kernel,
        grid=(n, k // tk),
        in_specs=[pl.BlockSpec((1, tk, f2), lambda e, i: (e, i, 0)), _const_spec(perm.shape)],
        out_specs=[out_spec, out_spec],
        out_shape=[jax.ShapeDtypeStruct((n, k, f), BF16)] * 2,
        compiler_params=_cparams(2),
        name="deinterleave_gate_up",
    )(w_gu, perm)


def _expert_kernel(be_ref, nu_ref, x_ref, wg_ref, wu_ref, bg_ref, bu_ref, wd_ref, bd_ref, y_ref):
    i = pl.program_id(0)

    @pl.when(i < nu_ref[0])
    def _():
        x = x_ref[...]
        hg = _dot(x, wg_ref[0]) + bg_ref[0]
        hu = _dot(x, wu_ref[0]) + bu_ref[0]
        gate = jnp.minimum(hg, SWIGLU_LIMIT)
        up = jnp.clip(hu, -SWIGLU_LIMIT, SWIGLU_LIMIT)
        glu = gate * _sigmoid(gate * SWIGLU_ALPHA)
        act = ((up + 1.0) * glu).astype(BF16)
        y_ref[...] = (_dot(act, wd_ref[0]) + bd_ref[0]).astype(y_ref.dtype)

    @pl.when(i >= nu_ref[0])
    def _():
        y_ref[...] = jnp.zeros_like(y_ref)


def _expert_ffn(xs, block_e, n_used, w_g, w_u, b_g, b_u, w_d, b_d, tm):
    p = xs.shape[0]
    nb = p // tm
    f = w_g.shape[2]
    w_spec = lambda s: pl.BlockSpec((1,) + s, lambda i, be, nu: (be[i], 0, 0))
    grid_spec = pltpu.PrefetchScalarGridSpec(
        num_scalar_prefetch=2,
        grid=(nb,),
        in_specs=[pl.BlockSpec((tm, D_MODEL), lambda i, be, nu: (i, 0)),
                  w_spec((D_MODEL, f)), w_spec((D_MODEL, f)), w_spec((1, f)), w_spec((1, f)),
                  w_spec((f, D_MODEL)), w_spec((1, D_MODEL))],
        out_specs=pl.BlockSpec((tm, D_MODEL), lambda i, be, nu: (i, 0)),
    )
    return pl.pallas_call(
        _expert_kernel,
        grid_spec=grid_spec,
        out_shape=jax.ShapeDtypeStruct((p, D_MODEL), BF16),
        compiler_params=_cparams(1),
        name="expert_ffn",
    )(block_e, n_used, xs, w_g, w_u, b_g, b_u, w_d, b_d)


def _combine_kernel(yg_ref, gate_ref, x_ref, g_ref, b_ref, o_ref, *, alpha):
    tm = x_ref.shape[0]
    lane = lax.broadcasted_iota(jnp.int32, (tm, LANES), 1)
    gd = gate_ref[...]
    m = jnp.zeros((tm, D_MODEL), F32)
    for kk in range(TOP_K):
        gk = jnp.sum(jnp.where(lane == kk, gd, 0.0), axis=-1, keepdims=True)
        m = m + gk * yg_ref[kk].astype(F32)
    o_ref[...] = _layer_norm(alpha * x_ref[...] + m, g_ref[...], b_ref[...])


def _combine(yg, gates_dense, x2d, ln_g, ln_b, alpha, tm):
    m = x2d.shape[0]
    x_spec = pl.BlockSpec((tm, D_MODEL), lambda i: (i, 0))
    return pl.pallas_call(
        functools.partial(_combine_kernel, alpha=alpha),
        grid=(m // tm,),
        in_specs=[pl.BlockSpec((TOP_K, tm, D_MODEL), lambda i: (0, i, 0)),
                  pl.BlockSpec((tm, LANES), lambda i: (i, 0)), x_spec,
                  _const_spec(ln_g.shape), _const_spec(ln_b.shape)],
        out_specs=x_spec,
        out_shape=jax.ShapeDtypeStruct((m, D_MODEL), F32),
        compiler_params=_cparams(1),
        name="combine_ln",
    )(yg, gates_dense, x2d, ln_g, ln_b)


def _tile(n, pref):
    t = min(n, pref)
    assert n % t == 0
    return t


def _pad_rows(w, rows, at):
    out = jnp.zeros((rows, w.shape[1]), w.dtype)
    return out.at[at:at + w.shape[0]].set(w)


def _moe(x1, router_w, router_b, w_g, w_u, b_g, b_u, w_d, b_d, expert_base, ln_g, ln_b, alpha, tm_tok, tm_exp):
    n_tok = x1.shape[0]
    rw_pad = jnp.zeros((D_MODEL, LANES), F32).at[:, :N_EXPERTS].set(router_w)
    rb_pad = jnp.zeros((1, LANES), F32).at[0, :N_EXPERTS].set(router_b)
    idx_d, gate_d, rank_d, counts_d = _route(x1, rw_pad, rb_pad, tm_tok)
    top_idx = idx_d[:, :TOP_K]
    rank = rank_d[:, :TOP_K].astype(jnp.int32)
    counts = counts_d[0, :N_EXPERTS].astype(jnp.int32)
    padded = (counts + tm_exp - 1) // tm_exp * tm_exp
    pad_end = jnp.cumsum(padded)
    pad_start = pad_end - padded
    dest = pad_start[top_idx] + rank
    n_blocks = -(-(n_tok * TOP_K) // tm_exp) + N_EXPERTS
    n_slots = n_blocks * tm_exp
    n_used = (pad_end[-1] // tm_exp).astype(jnp.int32)
    blk = jnp.arange(n_blocks, dtype=jnp.int32)
    blk_row = jnp.minimum(blk, n_used - 1) * tm_exp
    block_e = jnp.sum((pad_end[None, :] <= blk_row[:, None]).astype(jnp.int32), axis=1)
    block_e = jnp.minimum(block_e, N_EXPERTS - 1) + expert_base
    tok = jnp.broadcast_to(jnp.arange(n_tok, dtype=jnp.int32)[:, None], (n_tok, TOP_K))
    slot_tok = jnp.full((n_slots,), n_tok, jnp.int32).at[dest.reshape(-1)].set(tok.reshape(-1))
    x_pad = jnp.concatenate([x1.astype(BF16), jnp.zeros((1, D_MODEL), BF16)], axis=0)
    xs = x_pad[slot_tok]
    yb = _expert_ffn(xs, block_e, n_used.reshape(1), w_g, w_u, b_g, b_u, w_d, b_d, tm_exp)
    yg = yb[dest.T]
    return _combine(yg, gate_d, x1, ln_g, ln_b, alpha, tm_tok)


def kernel(x, mem, w_in, gla_conv, gla_gate_w2, gla_gate_b, gla_norm_g, hg_lower_bound, hg_norm_g, rw_mu, rw_w0, rw_w2, rw_a0, rw_a2, rw_g2, rw_k_k, rw_k_a, rw_r_k, rw_ln_w, rw_ln_b, w_mem_k, w_mem_v, w_out, ln1_g, ln1_b, router_w, router_b, w_gu, b_gu, w_down, b_down, ln2_g, ln2_b):
    batch, seq, d = x.shape
    depth = w_in.shape[0]
    n_mem = mem.shape[1]
    n_tok = batch * seq
    alpha = float((2.0 * depth) ** 0.25)
    tt = _tile(seq, 256)
    tm = _tile(n_tok, 512)
    tm_exp = 512

    lb = jax.nn.softmax(hg_lower_bound.astype(F32), axis=0)
    lb = jnp.cumsum(lb, axis=0) - lb[0]

    hv = _head_of(MEM_W, HEAD_DIM)
    head_mask = jnp.asarray((hv[None, :] == np.arange(N_HEADS)[:, None]).astype(np.float32))

    c0 = GLA_COLS
    c1 = c0 + HG_COLS
    c2 = c1 + RW_COLS
    widths = (GLA_PAD, HG_COLS, RW_COLS, MEM_W)
    xc = x.reshape(n_tok, d)
    mem2d = mem.reshape(batch * n_mem, d)
    row = lambda v: v.reshape(1, -1).astype(F32)

    n_le = depth * N_EXPERTS
    f2 = w_gu.shape[-1]
    w_g_all, w_u_all = _deinterleave_gate_up(w_gu.reshape(n_le, d, f2), 512)
    b_gu_all = b_gu.reshape(n_le, 1, f2)
    b_g_all = b_gu_all[:, :, 0::2]
    b_u_all = b_gu_all[:, :, 1::2]
    w_d_all = w_down.reshape(n_le, w_down.shape[2], d).astype(BF16)
    b_d_all = b_down.reshape(n_le, 1, d)
    for l in range(depth):
        wl = w_in[l]
        w_gla = jnp.concatenate([wl[:, 0:GLA_CONV_C], wl[:, GLA_CONV_C + GLA_GATE_RANK:c0],
                                 wl[:, GLA_CONV_C:GLA_CONV_C + GLA_GATE_RANK],
                                 jnp.zeros((d, GLA_PAD - GLA_COLS), F32)], axis=1)
        w_all = jnp.concatenate([w_gla, wl[:, c0:]], axis=1).astype(BF16)
        p_gla, p_hg, p_rw, q_mem = _project(xc, w_all, widths, tm)

        w2pad = _pad_rows(gla_gate_w2[l], GLA_PAD - GLA_CONV_C - GLA_V, 0)
        o_gla = _gla_group(p_gla, gla_conv[l], w2pad, row(gla_gate_b[l]), row(gla_norm_g[l]), batch, seq, tt)
        o_hg = _hgrn_group(p_hg, row(lb[l]), row(hg_norm_g[l]), batch, seq, tt)
        rw_params = (row(rw_mu[l]), row(rw_w0[l]), _pad_rows(rw_w2[l], LANES, 0), row(rw_a0[l]),
                     _pad_rows(rw_a2[l], LANES, RW_DECAY_RANK), rw_g2[l].astype(BF16), row(rw_k_k[l]),
                     row(rw_k_a[l]), row(rw_r_k[l]), row(rw_ln_w[l]), row(rw_ln_b[l]))
        o_rw = _rwkv_group(p_rw, rw_params, batch, seq, tt)

        w_kv = jnp.concatenate([w_mem_k[l], w_mem_v[l]], axis=1).astype(BF16)
        mk, mv = _project(mem2d, w_kv, (MEM_W, MEM_W), _tile(batch * n_mem, 512))
        k_exp = (mk.reshape(batch, 1, n_mem, MEM_W) * head_mask[None, :, None, :]).astype(BF16)
        v_exp = (mv.reshape(batch, 1, n_mem, MEM_W) * head_mask[None, :, None, :]).astype(BF16)
        o_mem = _mem_group(q_mem, k_exp, v_exp, batch, seq, _tile(seq, 512))

        x1 = _out_project((o_gla, o_hg, o_rw, o_mem), xc, w_out[l].astype(BF16), row(ln1_g[l]), row(ln1_b[l]), alpha, tm)

        xc = _moe(x1, router_w[l], router_b[l], w_g_all, w_u_all, b_g_all, b_u_all, w_d_all, b_d_all,
                  l * N_EXPERTS, row(ln2_g[l]), row(ln2_b[l]), alpha, tm, tm_exp)
    return xc.reshape(batch, seq, d)
```

```python
import functools

import numpy as np
import jax
import jax.numpy as jnp
from jax import lax
from jax.experimental import pallas as pl
from jax.experimental.pallas import tpu as pltpu

F32 = jnp.float32
BF16 = jnp.bfloat16
HIGHEST = lax.Precision.HIGHEST

D_MODEL = 1024
HEAD_DIM = 64
N_HEADS = 4
GROUP_W = 256
GLA_DK = 32
GLA_QK = 128
GLA_V = 256
GLA_GATE_RANK = 16
GLA_TAU = 16.0
CONV_W = 4
GLA_CONV_C = 2 * GLA_QK + GLA_V
GLA_COLS = 784
GLA_PAD = 896
HG_COLS = 1024
RW_W = 256
RW_DECAY_RANK = 64
RW_A_RANK = 64
RW_GATE_RANK = 128
RW_COLS = 1024
RW_LN_EPS = 64e-5
MEM_W = 256
N_EXPERTS = 32
TOP_K = 4
SWIGLU_ALPHA = 1.702
SWIGLU_LIMIT = 7.0
LN_EPS = 1e-5
RMS_EPS = 1e-6
GATE_FLOOR = 1e-30
LOG2E = 1.4426950408889634
NEG_BIG = -1.0e30

LANES = 128
SUBLANES = 8
SUB_CHUNK = 16
RW_CHUNK = 32
VMEM_LIMIT = 48 * 1024 * 1024


def _cparams(n_axes):
    return pltpu.CompilerParams(dimension_semantics=("arbitrary",) * n_axes, vmem_limit_bytes=VMEM_LIMIT)


def _dot(a, b, precision=None):
    return jnp.dot(a, b, preferred_element_type=F32, precision=precision)


def _dot_nt(a, b):
    return lax.dot_general(a, b, (((1,), (1,)), ((), ())), preferred_element_type=F32)


def _dot_tn(a, b):
    return lax.dot_general(a, b, (((0,), (0,)), ((), ())), preferred_element_type=F32)


def _split_bf16(x, terms):
    parts = []
    for _ in range(terms):
        p = x.astype(BF16)
        parts.append(p)
        x = x - p.astype(F32)
    return parts


def _dot_onehot_rhs(x, e_bf16, terms):
    parts = _split_bf16(x, terms)
    acc = _dot(parts[0], e_bf16)
    for p in parts[1:]:
        acc = acc + _dot(p, e_bf16)
    return acc


def _dot_onehot_lhs(e_bf16, x, terms):
    parts = _split_bf16(x, terms)
    acc = _dot(e_bf16, parts[0])
    for p in parts[1:]:
        acc = acc + _dot(e_bf16, p)
    return acc


def _sigmoid(x):
    return 1.0 / (1.0 + jnp.exp(-x))


def _silu(x):
    return x * _sigmoid(x)


def _softplus(x):
    return jnp.maximum(x, 0.0) + jnp.log(1.0 + jnp.exp(-jnp.abs(x)))


def _const_spec(shape):
    nd = len(shape)
    return pl.BlockSpec(shape, lambda *_: (0,) * nd)


def _proj_kernel(x_ref, w_ref, *out_refs, widths):
    xb = x_ref[...].astype(BF16)
    off = 0
    for o_ref, wd in zip(out_refs, widths):
        o_ref[...] = _dot(xb, w_ref[:, off:off + wd]).astype(o_ref.dtype)
        off += wd


def _project(x2d, w_bf16, widths, tm):
    m, k = x2d.shape
    n = w_bf16.shape[1]
    assert sum(widths) == n and m % tm == 0
    return pl.pallas_call(
        functools.partial(_proj_kernel, widths=tuple(widths)),
        grid=(m // tm,),
        in_specs=[pl.BlockSpec((tm, k), lambda i: (i, 0)), _const_spec((k, n))],
        out_specs=[pl.BlockSpec((tm, wd), lambda i: (i, 0)) for wd in widths],
        out_shape=[jax.ShapeDtypeStruct((m, wd), F32) for wd in widths],
        compiler_params=_cparams(1),
        name="in_proj",
    )(x2d, w_bf16)


def _gla_core(q, k, v, g, lt_ref, ekv_ref, bdt_ref, st_ref):
    tt, ck = q.shape
    cv = v.shape[1]
    nc = tt // SUB_CHUNK
    b = _dot_onehot_lhs(lt_ref[...], g * LOG2E, 3)
    q3 = q.reshape(nc, SUB_CHUNK, ck)
    k3 = k.reshape(nc, SUB_CHUNK, ck)
    v3 = v.reshape(nc, SUB_CHUNK, cv)
    b3 = b.reshape(nc, SUB_CHUNK, ck)
    row = lax.broadcasted_iota(jnp.int32, (nc, SUB_CHUNK, ck), 1)
    ekv = ekv_ref[...]

    o3 = jnp.zeros((nc, SUB_CHUNK, cv), F32)
    for j in range(SUB_CHUNK):
        rel = jnp.where(row >= j, b3 - b3[:, j:j + 1, :], NEG_BIG)
        w = q3 * jnp.exp2(rel) * k3[:, j:j + 1, :]
        s = _dot(w.reshape(tt, ck).astype(BF16), ekv).reshape(nc, SUB_CHUNK, cv)
        o3 = o3 + s * v3[:, j:j + 1, :]

    bend3 = b3[:, SUB_CHUNK - 1:SUB_CHUNK, :]
    qg = (q3 * jnp.exp2(b3)).reshape(tt, ck).astype(BF16)
    kg = (k3 * jnp.exp2(bend3 - b3)).reshape(tt, ck).astype(BF16)
    dec = jnp.exp2(bend3)
    vb = v.astype(BF16)
    bdt = bdt_ref[...]
    st = st_ref[...]
    outs = []
    for c in range(nc):
        sl = slice(c * SUB_CHUNK, (c + 1) * SUB_CHUNK)
        outs.append(_dot_nt(qg[sl], st.astype(BF16)))
        st = st * dec[c] + bdt * _dot_tn(vb[sl], kg[sl])
    st_ref[...] = st
    return o3.reshape(tt, cv) + jnp.concatenate(outs, axis=0)


def _head_rms_gate(o, evv, norm_g, gate):
    ms = _dot((o * o).astype(BF16), evv) * (1.0 / HEAD_DIM)
    return o * lax.rsqrt(ms + RMS_EPS) * norm_g * _silu(gate)


def _gla_kernel(p_ref, conv_ref, w2_ref, gb_ref, ng_ref, lt_ref, ekv_ref, bdt_ref, evv_ref,
                o_ref, xp_ref, st_ref, *, tt):
    t = pl.program_id(1)

    @pl.when(t == 0)
    def _():
        st_ref[...] = jnp.zeros_like(st_ref)
        xp_ref[tt:tt + SUBLANES, :] = jnp.zeros((SUBLANES, GLA_CONV_C), F32)

    xp_ref[0:SUBLANES, :] = xp_ref[tt:tt + SUBLANES, :]
    xp_ref[SUBLANES:tt + SUBLANES, :] = p_ref[:, 0:GLA_CONV_C]
    acc = jnp.zeros((tt, GLA_CONV_C), F32)
    for i in range(CONV_W):
        shift = CONV_W - 1 - i
        acc = acc + xp_ref[SUBLANES - shift:SUBLANES - shift + tt, :] * conv_ref[i:i + 1, :]
    qkv = _silu(acc)
    q = qkv[:, 0:GLA_QK] * (GLA_DK ** -0.5)
    k = qkv[:, GLA_QK:2 * GLA_QK]
    v = qkv[:, 2 * GLA_QK:GLA_CONV_C]
    g_out = p_ref[:, GLA_CONV_C:GLA_CONV_C + GLA_V]
    z = _dot(p_ref[:, GLA_CONV_C + GLA_V:GLA_PAD], w2_ref[...], precision=HIGHEST) + gb_ref[...]
    log_alpha = -_softplus(-z) * (1.0 / GLA_TAU)
    o = _gla_core(q, k, v, log_alpha, lt_ref, ekv_ref, bdt_ref, st_ref)
    o_ref[...] = _head_rms_gate(o, evv_ref[...], ng_ref[...], g_out).astype(o_ref.dtype)


def _hgrn_kernel(p_ref, lb_ref, ng_ref, lt_ref, ekv_ref, bdt_ref, evv_ref, o_ref, st_ref):
    t = pl.program_id(1)

    @pl.when(t == 0)
    def _():
        st_ref[...] = jnp.zeros_like(st_ref)

    hq = p_ref[:, 0:256]
    hf = p_ref[:, 256:512]
    hi = p_ref[:, 512:768]
    hgate = p_ref[:, 768:1024]
    lb = lb_ref[...]
    f_gate = lb + (1.0 - lb) * _sigmoid(hf)
    log_f = jnp.log(jnp.maximum(f_gate, GATE_FLOOR))
    k_in = (1.0 - lb) * _sigmoid(-hf)
    o = _gla_core(_silu(hq), k_in, hi, log_f, lt_ref, ekv_ref, bdt_ref, st_ref)
    o_ref[...] = _head_rms_gate(o, evv_ref[...], ng_ref[...], hgate).astype(o_ref.dtype)


def _head_of(n, width):
    return np.arange(n) // width


def _gla_consts(tt, ck, cv):
    r = np.arange(tt)
    lt = ((r[:, None] // SUB_CHUNK == r[None, :] // SUB_CHUNK) & (r[None, :] <= r[:, None])).astype(np.float32)
    hk = _head_of(ck, ck // N_HEADS)
    hv = _head_of(cv, cv // N_HEADS)
    ekv = (hk[:, None] == hv[None, :]).astype(np.float32)
    evv = (hv[:, None] == hv[None, :]).astype(np.float32)
    return (jnp.asarray(lt, BF16), jnp.asarray(ekv, BF16), jnp.asarray(ekv.T, F32), jnp.asarray(evv, BF16))


def _gla_group(p_gla, conv_w, w2pad, gate_b, norm_g, batch, seq, tt):
    lt, ekv, bdt, evv = _gla_consts(tt, GLA_QK, GLA_V)
    nt = seq // tt
    row_spec = lambda w: pl.BlockSpec((tt, w), lambda b, t: (b * nt + t, 0))
    return pl.pallas_call(
        functools.partial(_gla_kernel, tt=tt),
        grid=(batch, nt),
        in_specs=[row_spec(GLA_PAD), _const_spec(conv_w.shape), _const_spec(w2pad.shape),
                  _const_spec(gate_b.shape), _const_spec(norm_g.shape), _const_spec(lt.shape),
                  _const_spec(ekv.shape), _const_spec(bdt.shape), _const_spec(evv.shape)],
        out_specs=row_spec(GLA_V),
        out_shape=jax.ShapeDtypeStruct((batch * seq, GLA_V), BF16),
        scratch_shapes=[pltpu.VMEM((tt + 2 * SUBLANES, GLA_CONV_C), F32), pltpu.VMEM((GLA_V, GLA_QK), F32)],
        compiler_params=_cparams(2),
        name="gla_group",
    )(p_gla, conv_w, w2pad, gate_b, norm_g, lt, ekv, bdt, evv)


def _hgrn_group(p_hg, lb, norm_g, batch, seq, tt):
    lt, ekv, bdt, evv = _gla_consts(tt, 256, 256)
    nt = seq // tt
    row_spec = lambda w: pl.BlockSpec((tt, w), lambda b, t: (b * nt + t, 0))
    return pl.pallas_call(
        _hgrn_kernel,
        grid=(batch, nt),
        in_specs=[row_spec(HG_COLS), _const_spec(lb.shape), _const_spec(norm_g.shape), _const_spec(lt.shape),
                  _const_spec(ekv.shape), _const_spec(bdt.shape), _const_spec(evv.shape)],
        out_specs=row_spec(256),
        out_shape=jax.ShapeDtypeStruct((batch * seq, 256), BF16),
        scratch_shapes=[pltpu.VMEM((256, 256), F32)],
        compiler_params=_cparams(2),
        name="hgrn_group",
    )(p_hg, lb, norm_g, lt, ekv, bdt, evv)


def _expand_heads(x, hm):
    return jnp.concatenate([x * hm[h:h + 1, :] for h in range(N_HEADS)], axis=0)


def _fold_heads(x_e, c):
    out = x_e[0:c]
    for h in range(1, N_HEADS):
        out = out + x_e[h * c:(h + 1) * c]
    return out


def _rwkv_kernel(p_ref, mu_ref, w0_ref, w2_ref, a0_ref, a2_ref, g2_ref, kk_ref, ka_ref, rk_ref,
                 lnw_ref, lnb_ref, tri_ref, hm_ref, sl_ref, il_ref, eye_ref, bd_ref, evv_ref,
                 o_ref, xp_ref, st_ref, *, tt):
    t = pl.program_id(1)

    @pl.when(t == 0)
    def _():
        st_ref[...] = jnp.zeros_like(st_ref)
        xp_ref[tt:tt + SUBLANES, :] = jnp.zeros((SUBLANES, RW_COLS), F32)

    xp_ref[0:SUBLANES, :] = xp_ref[tt:tt + SUBLANES, :]
    xp_ref[SUBLANES:tt + SUBLANES, :] = p_ref[...]
    cur = p_ref[...]
    prev = xp_ref[SUBLANES - 1:SUBLANES - 1 + tt, :]
    p = cur + (prev - cur) * mu_ref[...]
    rr = p[:, 0:256]
    rk = p[:, 256:512]
    rv = p[:, 512:768]
    lr = p[:, 768:896]
    g_lr = p[:, 896:1024]
    w_log = -_softplus(-(w0_ref[...] + _dot(jnp.tanh(lr), w2_ref[...], precision=HIGHEST))) - 0.5
    lw = -jnp.exp(w_log)
    a = _sigmoid(a0_ref[...] + _dot(lr, a2_ref[...], precision=HIGHEST))
    g = _dot(_sigmoid(g_lr).astype(BF16), g2_ref[...])
    evv = evv_ref[...]
    kk = rk * kk_ref[...]
    kk_sq = _dot_onehot_rhs(kk * kk, evv, 2)
    kk = kk / jnp.maximum(jnp.sqrt(kk_sq), 1e-12)
    hk = rk * (1.0 + (a - 1.0) * ka_ref[...])
    a_vec = -kk
    b_vec = kk * a

    hm = hm_ref[...]
    sl_m = sl_ref[...]
    il_m = il_ref[...]
    bd = bd_ref[...]
    eye = eye_ref[...]
    c = RW_CHUNK
    nc = tt // c
    n = N_HEADS * c

    cs = _dot_onehot_lhs(tri_ref[...], lw, 3)
    igam = jnp.exp(-cs)
    at = a_vec * jnp.exp(cs - lw)
    bh = b_vec * igam
    kh = hk * igam
    rt = rr * jnp.exp(cs)
    gam_end = jnp.exp(cs.reshape(nc, c, RW_W)[:, c - 1:c, :])
    bhg = (bh.reshape(nc, c, RW_W) * gam_end).reshape(tt, RW_W)
    khg = (kh.reshape(nc, c, RW_W) * gam_end).reshape(tt, RW_W)

    def expand(x):
        return _expand_heads(x, hm).astype(BF16)

    chunks = [slice(ci * c, (ci + 1) * c) for ci in range(nc)]
    at_e = [expand(at[s_]) for s_ in chunks]
    v_e = [expand(rv[s_]) for s_ in chunks]
    l_ab, l_ak_rk, l_rb = [], [], []
    for ci, s_ in enumerate(chunks):
        lhs = jnp.concatenate([at_e[ci], expand(rt[s_])], axis=0)
        rhs = jnp.concatenate([expand(bh[s_]), expand(kh[s_])], axis=0)
        l_all = _dot_nt(lhs, rhs)
        l_ab.append(l_all[0:n, 0:n] * sl_m)
        l_rb.append((l_all[n:2 * n, 0:n] * il_m).astype(BF16))
        l_ak_rk.append(jnp.concatenate([l_all[0:n, n:2 * n] * sl_m, l_all[n:2 * n, n:2 * n] * il_m],
                                       axis=0).astype(BF16))
    tms = [eye + l for l in l_ab]
    pws = l_ab
    for _ in range(int(np.log2(c)) - 1):
        pws = [_dot(p.astype(BF16), p.astype(BF16)) for p in pws]
        tms = [tm + _dot(tm.astype(BF16), p.astype(BF16)) for tm, p in zip(tms, pws)]
    w_c, uv_c, yv_c = [], [], []
    for ci in range(nc):
        lv = _dot(l_ak_rk[ci], v_e[ci])
        tw = _dot(tms[ci].astype(BF16), jnp.concatenate([at_e[ci], lv[0:n].astype(BF16)], axis=1))
        w_c.append(_fold_heads(tw[:, 0:RW_W], c))
        uv_c.append(_fold_heads(tw[:, RW_W:2 * RW_W], c))
        yv_c.append(_fold_heads(lv[n:2 * n], c))

    st = st_ref[...]
    ys = []
    for ci, s_ in enumerate(chunks):
        wr = _dot_nt(jnp.concatenate([w_c[ci], rt[s_]], axis=0).astype(BF16), st.astype(BF16))
        u = wr[0:c] + uv_c[ci]
        ys.append(wr[c:2 * c] + _fold_heads(_dot(l_rb[ci], expand(u)), c) + yv_c[ci])
        lhs = jnp.concatenate([u, rv[s_]], axis=0).astype(BF16)
        rhs = jnp.concatenate([bhg[s_], khg[s_]], axis=0).astype(BF16)
        st = st * gam_end[ci] + bd * _dot_tn(lhs, rhs)
    st_ref[...] = st
    y = jnp.concatenate(ys, axis=0)

    inv_n = 1.0 / HEAD_DIM
    mu = _dot(y.astype(BF16), evv) * inv_n
    yc = y - mu
    var = _dot((yc * yc).astype(BF16), evv) * inv_n
    y = yc * lax.rsqrt(var + RW_LN_EPS) * lnw_ref[...] + lnb_ref[...]
    bonus = _dot_onehot_rhs(rr * hk * rk_ref[...], evv, 2) * rv
    o_ref[...] = ((y + bonus) * g).astype(o_ref.dtype)


def _rwkv_consts(tt):
    c = RW_CHUNK
    n = N_HEADS * c
    r = np.arange(n)
    same = (r[:, None] // c) == (r[None, :] // c)
    sl = same & ((r[None, :] % c) < (r[:, None] % c))
    il = same & ((r[None, :] % c) <= (r[:, None] % c))
    q = np.arange(tt)
    tri = ((q[:, None] // c == q[None, :] // c) & (q[None, :] <= q[:, None])).astype(np.float32)
    hv = _head_of(RW_W, HEAD_DIM)
    hm = (hv[None, :] == np.arange(N_HEADS)[:, None]).astype(np.float32)
    bd = (hv[:, None] == hv[None, :]).astype(np.float32)
    return (jnp.asarray(tri, BF16), jnp.asarray(hm, F32), jnp.asarray(sl, F32), jnp.asarray(il, F32),
            jnp.asarray(np.eye(n), F32), jnp.asarray(bd, F32), jnp.asarray(bd, BF16))


def _rwkv_group(p_rw, params, batch, seq, tt):
    consts = _rwkv_consts(tt)
    nt = seq // tt
    row_spec = lambda w: pl.BlockSpec((tt, w), lambda b, t: (b * nt + t, 0))
    ops = tuple(params) + consts
    return pl.pallas_call(
        functools.partial(_rwkv_kernel, tt=tt),
        grid=(batch, nt),
        in_specs=[row_spec(RW_COLS)] + [_const_spec(o.shape) for o in ops],
        out_specs=row_spec(RW_W),
        out_shape=jax.ShapeDtypeStruct((batch * seq, RW_W), BF16),
        scratch_shapes=[pltpu.VMEM((tt + 2 * SUBLANES, RW_COLS), F32), pltpu.VMEM((RW_W, RW_W), F32)],
        compiler_params=_cparams(2),
        name="rwkv_group",
    )(p_rw, *ops)


def _mem_kernel(q_ref, k_ref, v_ref, o_ref):
    q = (q_ref[...] * (HEAD_DIM ** -0.5)).astype(BF16)
    acc = jnp.zeros(o_ref.shape, F32)
    for h in range(N_HEADS):
        s = _dot_nt(q, k_ref[0, h])
        e = jnp.exp(s - jnp.max(s, axis=-1, keepdims=True))
        l = jnp.sum(e, axis=-1, keepdims=True)
        acc = acc + _dot(e.astype(BF16), v_ref[0, h]) / l
    o_ref[...] = acc.astype(o_ref.dtype)


def _mem_group(q_mem, k_exp, v_exp, batch, seq, tt):
    nt = seq // tt
    n_mem = k_exp.shape[2]
    kv_spec = pl.BlockSpec((1, N_HEADS, n_mem, MEM_W), lambda b, t: (b, 0, 0, 0))
    row_spec = pl.BlockSpec((tt, MEM_W), lambda b, t: (b * nt + t, 0))
    return pl.pallas_call(
        _mem_kernel,
        grid=(batch, nt),
        in_specs=[row_spec, kv_spec, kv_spec],
        out_specs=row_spec,
        out_shape=jax.ShapeDtypeStruct((batch * seq, MEM_W), BF16),
        compiler_params=_cparams(2),
        name="mem_attn",
    )(q_mem, k_exp, v_exp)


def _layer_norm(z, g, b):
    mu = jnp.mean(z, axis=-1, keepdims=True)
    zc = z - mu
    var = jnp.mean(zc * zc, axis=-1, keepdims=True)
    return zc * lax.rsqrt(var + LN_EPS) * g + b


def _outproj_kernel(oa_ref, ob_ref, oc_ref, od_ref, x_ref, w_ref, g_ref, b_ref, y_ref, *, alpha):
    h = _dot(oa_ref[...], w_ref[0:256, :])
    h = h + _dot(ob_ref[...], w_ref[256:512, :])
    h = h + _dot(oc_ref[...], w_ref[512:768, :])
    h = h + _dot(od_ref[...], w_ref[768:1024, :])
    y_ref[...] = _layer_norm(alpha * x_ref[...] + h, g_ref[...], b_ref[...])


def _out_project(groups, x2d, w_out, ln_g, ln_b, alpha, tm):
    m = x2d.shape[0]
    g_spec = pl.BlockSpec((tm, GROUP_W), lambda i: (i, 0))
    x_spec = pl.BlockSpec((tm, D_MODEL), lambda i: (i, 0))
    return pl.pallas_call(
        functools.partial(_outproj_kernel, alpha=alpha),
        grid=(m // tm,),
        in_specs=[g_spec] * 4 + [x_spec, _const_spec(w_out.shape), _const_spec(ln_g.shape), _const_spec(ln_b.shape)],
        out_specs=x_spec,
        out_shape=jax.ShapeDtypeStruct((m, D_MODEL), F32),
        compiler_params=_cparams(1),
        name="out_proj_ln",
    )(*groups, x2d, w_out, ln_g, ln_b)


def _router_kernel(x_ref, w_ref, b_ref, tri_ref, idx_ref, gate_ref, rank_ref, cnt_ref, carry_ref):
    i = pl.program_id(0)

    @pl.when(i == 0)
    def _():
        carry_ref[...] = jnp.zeros_like(carry_ref)

    tm = x_ref.shape[0]
    lane = lax.broadcasted_iota(jnp.int32, (tm, LANES), 1)
    logits = _dot(x_ref[...], w_ref[...], precision=HIGHEST) + b_ref[...]
    neg = jnp.float32(-3.0e38)
    work = jnp.where(lane < N_EXPERTS, logits, neg)
    vals, idxs = [], []
    sel = jnp.zeros((tm, LANES), F32)
    for _ in range(TOP_K):
        m = jnp.max(work, axis=-1, keepdims=True)
        idx = jnp.min(jnp.where(work == m, lane, LANES), axis=-1, keepdims=True)
        hit = lane == idx
        sel = jnp.where(hit, 1.0, sel)
        work = jnp.where(hit, neg, work)
        vals.append(m)
        idxs.append(idx)
    es = [jnp.exp(v - vals[0]) for v in vals]
    den = es[0] + es[1] + es[2] + es[3]
    incl = _dot(tri_ref[...], sel.astype(BF16))
    rank_dense = incl - sel + carry_ref[...]
    carry_ref[...] = carry_ref[...] + incl[tm - 1:tm, :]
    cnt_ref[...] = carry_ref[...]
    idx_out = jnp.zeros((tm, LANES), jnp.int32)
    gate_out = jnp.zeros((tm, LANES), F32)
    rank_out = jnp.zeros((tm, LANES), F32)
    for kk in range(TOP_K):
        rk = jnp.sum(jnp.where(lane == idxs[kk], rank_dense, 0.0), axis=-1, keepdims=True)
        here = lane == kk
        idx_out = jnp.where(here, idxs[kk], idx_out)
        gate_out = jnp.where(here, es[kk] / den, gate_out)
        rank_out = jnp.where(here, rk, rank_out)
    idx_ref[...] = idx_out
    gate_ref[...] = gate_out
    rank_ref[...] = rank_out


def _route(x2d, rw_pad, rb_pad, tm):
    m = x2d.shape[0]
    tri = jnp.asarray(np.tril(np.ones((tm, tm), np.float32)), BF16)
    row = pl.BlockSpec((tm, LANES), lambda i: (i, 0))
    return pl.pallas_call(
        _router_kernel,
        grid=(m // tm,),
        in_specs=[pl.BlockSpec((tm, D_MODEL), lambda i: (i, 0)), _const_spec(rw_pad.shape),
                  _const_spec(rb_pad.shape), _const_spec(tri.shape)],
        out_specs=[row, row, row, _const_spec((1, LANES))],
        out_shape=[jax.ShapeDtypeStruct((m, LANES), jnp.int32), jax.ShapeDtypeStruct((m, LANES), F32),
                   jax.ShapeDtypeStruct((m, LANES), F32), jax.ShapeDtypeStruct((1, LANES), F32)],
        scratch_shapes=[pltpu.VMEM((1, LANES), F32)],
        compiler_params=_cparams(1),
        name="router",
    )(x2d, rw_pad, rb_pad, tri)


DEINT_W = 2 * LANES


def _deinterleave_kernel(w_ref, perm_ref, wg_ref, wu_ref):
    perm = perm_ref[...]
    for blk in range(w_ref.shape[2] // DEINT_W):
        wb = w_ref[0, :, blk * DEINT_W:(blk + 1) * DEINT_W].astype(BF16)
        r = _dot(wb, perm)
        wg_ref[0, :, blk * LANES:(blk + 1) * LANES] = r[:, 0:LANES].astype(BF16)
        wu_ref[0, :, blk * LANES:(blk + 1) * LANES] = r[:, LANES:DEINT_W].astype(BF16)


def _deinterleave_gate_up(w_gu, tk):
    n, k, f2 = w_gu.shape
    f = f2 // 2
    j = np.arange(DEINT_W)
    src = np.where(j < LANES, 2 * j, 2 * (j - LANES) + 1)
    perm = np.zeros((DEINT_W, DEINT_W), np.float32)
    perm[src, j] = 1.0
    perm = jnp.asarray(perm, BF16)
    out_spec = pl.BlockSpec((1, tk, f), lambda e, i: (e, i, 0))
    return pl.pallas_call(
        _deinterleave_kernel,
        grid=(n, k // tk),
        in_specs=[pl.BlockSpec((1, tk, f2), lambda e, i: (e, i, 0)), _const_spec(perm.shape)],
        out_specs=[out_spec, out_spec],
        out_shape=[jax.ShapeDtypeStruct((n, k, f), BF16)] * 2,
        compiler_params=_cparams(2),
        name="deinterleave_gate_up",
    )(w_gu, perm)


def _expert_kernel(be_ref, nu_ref, x_ref, wg_ref, wu_ref, bg_ref, bu_ref, wd_ref, bd_ref, y_ref):
    i = pl.program_id(0)

    @pl.when(i < nu_ref[0])
    def _():
        x = x_ref[...]
        hg = _dot(x, wg_ref[0]) + bg_ref[0]
        hu = _dot(x, wu_ref[0]) + bu_ref[0]
        gate = jnp.minimum(hg, SWIGLU_LIMIT)
        up = jnp.clip(hu, -SWIGLU_LIMIT, SWIGLU_LIMIT)
        glu = gate * _sigmoid(gate * SWIGLU_ALPHA)
        act = ((up + 1.0) * glu).astype(BF16)
        y_ref[...] = (_dot(act, wd_ref[0]) + bd_ref[0]).astype(y_ref.dtype)

    @pl.when(i >= nu_ref[0])
    def _():
        y_ref[...] = jnp.zeros_like(y_ref)


def _expert_ffn(xs, block_e, n_used, w_g, w_u, b_g, b_u, w_d, b_d, tm):
    p = xs.shape[0]
    nb = p // tm
    f = w_g.shape[2]
    w_spec = lambda s: pl.BlockSpec((1,) + s, lambda i, be, nu: (be[i], 0, 0))
    grid_spec = pltpu.PrefetchScalarGridSpec(
        num_scalar_prefetch=2,
        grid=(nb,),
        in_specs=[pl.BlockSpec((tm, D_MODEL), lambda i, be, nu: (i, 0)),
                  w_spec((D_MODEL, f)), w_spec((D_MODEL, f)), w_spec((1, f)), w_spec((1, f)),
                  w_spec((f, D_MODEL)), w_spec((1, D_MODEL))],
        out_specs=pl.BlockSpec((tm, D_MODEL), lambda i, be, nu: (i, 0)),
    )
    return pl.pallas_call(
        _expert_kernel,
        grid_spec=grid_spec,
        out_shape=jax.ShapeDtypeStruct((p, D_MODEL), F32),
        compiler_params=_cparams(1),
        name="expert_ffn",
    )(block_e, n_used, xs, w_g, w_u, b_g, b_u, w_d, b_d)


def _combine_kernel(yg_ref, gate_ref, x_ref, g_ref, b_ref, o_ref, *, alpha):
    tm = x_ref.shape[0]
    lane = lax.broadcasted_iota(jnp.int32, (tm, LANES), 1)
    gd = gate_ref[...]
    m = jnp.zeros((tm, D_MODEL), F32)
    for kk in range(TOP_K):
        gk = jnp.sum(jnp.where(lane == kk, gd, 0.0), axis=-1, keepdims=True)
        m = m + gk * yg_ref[kk].astype(F32)
    o_ref[...] = _layer_norm(alpha * x_ref[...] + m, g_ref[...], b_ref[...])


def _combine(yg, gates_dense, x2d, ln_g, ln_b, alpha, tm):
    m = x2d.shape[0]
    x_spec = pl.BlockSpec((tm, D_MODEL), lambda i: (i, 0))
    return pl.pallas_call(
        functools.partial(_combine_kernel, alpha=alpha),
        grid=(m // tm,),
        in_specs=[pl.BlockSpec((TOP_K, tm, D_MODEL), lambda i: (0, i, 0)),
                  pl.BlockSpec((tm, LANES), lambda i: (i, 0)), x_spec,
                  _const_spec(ln_g.shape), _const_spec(ln_b.shape)],
        out_specs=x_spec,
        out_shape=jax.ShapeDtypeStruct((m, D_MODEL), F32),
        compiler_params=_cparams(1),
        name="combine_ln",
    )(yg, gates_dense, x2d, ln_g, ln_b)


def _tile(n, pref):
    t = min(n, pref)
    assert n % t == 0
    return t


def _pad_rows(w, rows, at):
    out = jnp.zeros((rows, w.shape[1]), w.dtype)
    return out.at[at:at + w.shape[0]].set(w)


def _moe(x1, router_w, router_b, w_g, w_u, b_g, b_u, w_d, b_d, expert_base, ln_g, ln_b, alpha, tm_tok, tm_exp):
    n_tok = x1.shape[0]
    rw_pad = jnp.zeros((D_MODEL, LANES), F32).at[:, :N_EXPERTS].set(router_w)
    rb_pad = jnp.zeros((1, LANES), F32).at[0, :N_EXPERTS].set(router_b)
    idx_d, gate_d, rank_d, counts_d = _route(x1, rw_pad, rb_pad, tm_tok)
    top_idx = idx_d[:, :TOP_K]
    rank = rank_d[:, :TOP_K].astype(jnp.int32)
    counts = counts_d[0, :N_EXPERTS].astype(jnp.int32)
    padded = (counts + tm_exp - 1) // tm_exp * tm_exp
    pad_end = jnp.cumsum(padded)
    pad_start = pad_end - padded
    dest = pad_start[top_idx] + rank
    n_blocks = -(-(n_tok * TOP_K) // tm_exp) + N_EXPERTS
    n_slots = n_blocks * tm_exp
    n_used = (pad_end[-1] // tm_exp).astype(jnp.int32)
    blk = jnp.arange(n_blocks, dtype=jnp.int32)
    blk_row = jnp.minimum(blk, n_used - 1) * tm_exp
    block_e = jnp.sum((pad_end[None, :] <= blk_row[:, None]).astype(jnp.int32), axis=1)
    block_e = jnp.minimum(block_e, N_EXPERTS - 1) + expert_base
    tok = jnp.broadcast_to(jnp.arange(n_tok, dtype=jnp.int32)[:, None], (n_tok, TOP_K))
    slot_tok = jnp.full((n_slots,), n_tok, jnp.int32).at[dest.reshape(-1)].set(tok.reshape(-1))
    x_pad = jnp.concatenate([x1.astype(BF16), jnp.zeros((1, D_MODEL), BF16)], axis=0)
    xs = x_pad[slot_tok]
    yb = _expert_ffn(xs, block_e, n_used.reshape(1), w_g, w_u, b_g, b_u, w_d, b_d, tm_exp)
    yg = yb[dest.T]
    return _combine(yg, gate_d, x1, ln_g, ln_b, alpha, tm_tok)


def kernel(x, mem, w_in, gla_conv, gla_gate_w2, gla_gate_b, gla_norm_g, hg_lower_bound, hg_norm_g, rw_mu, rw_w0, rw_w2, rw_a0, rw_a2, rw_g2, rw_k_k, rw_k_a, rw_r_k, rw_ln_w, rw_ln_b, w_mem_k, w_mem_v, w_out, ln1_g, ln1_b, router_w, router_b, w_gu, b_gu, w_down, b_down, ln2_g, ln2_b):
    batch, seq, d = x.shape
    depth = w_in.shape[0]
    n_mem = mem.shape[1]
    n_tok = batch * seq
    alpha = float((2.0 * depth) ** 0.25)
    tt = _tile(seq, 256)
    tm = _tile(n_tok, 512)
    tm_exp = 512

    lb = jax.nn.softmax(hg_lower_bound.astype(F32), axis=0)
    lb = jnp.cumsum(lb, axis=0) - lb[0]

    hv = _head_of(MEM_W, HEAD_DIM)
    head_mask = jnp.asarray((hv[None, :] == np.arange(N_HEADS)[:, None]).astype(np.float32))

    c0 = GLA_COLS
    c1 = c0 + HG_COLS
    c2 = c1 + RW_COLS
    widths = (GLA_PAD, HG_COLS, RW_COLS, MEM_W)
    xc = x.reshape(n_tok, d)
    mem2d = mem.reshape(batch * n_mem, d)
    row = lambda v: v.reshape(1, -1).astype(F32)

    n_le = depth * N_EXPERTS
    f2 = w_gu.shape[-1]
    w_g_all, w_u_all = _deinterleave_gate_up(w_gu.reshape(n_le, d, f2), 512)
    b_gu_all = b_gu.reshape(n_le, 1, f2)
    b_g_all = b_gu_all[:, :, 0::2]
    b_u_all = b_gu_all[:, :, 1::2]
    w_d_all = w_down.reshape(n_le, w_down.shape[2], d).astype(BF16)
    b_d_all = b_down.reshape(n_le, 1, d)
    for l in range(depth):
        wl = w_in[l]
        w_gla = jnp.concatenate([wl[:, 0:GLA_CONV_C], wl[:, GLA_CONV_C + GLA_GATE_RANK:c0],
                                 wl[:, GLA_CONV_C:GLA_CONV_C + GLA_GATE_RANK],
                                 jnp.zeros((d, GLA_PAD - GLA_COLS), F32)], axis=1)
        w_all = jnp.concatenate([w_gla, wl[:, c0:]], axis=1).astype(BF16)
        p_gla, p_hg, p_rw, q_mem = _project(xc, w_all, widths, tm)

        w2pad = _pad_rows(gla_gate_w2[l], GLA_PAD - GLA_CONV_C - GLA_V, 0)
        o_gla = _gla_group(p_gla, gla_conv[l], w2pad, row(gla_gate_b[l]), row(gla_norm_g[l]), batch, seq, tt)
        o_hg = _hgrn_group(p_hg, row(lb[l]), row(hg_norm_g[l]), batch, seq, tt)
        rw_params = (row(rw_mu[l]), row(rw_w0[l]), _pad_rows(rw_w2[l], LANES, 0), row(rw_a0[l]),
                     _pad_rows(rw_a2[l], LANES, RW_DECAY_RANK), rw_g2[l].astype(BF16), row(rw_k_k[l]),
                     row(rw_k_a[l]), row(rw_r_k[l]), row(rw_ln_w[l]), row(rw_ln_b[l]))
        o_rw = _rwkv_group(p_rw, rw_params, batch, seq, tt)

        w_kv = jnp.concatenate([w_mem_k[l], w_mem_v[l]], axis=1).astype(BF16)
        mk, mv = _project(mem2d, w_kv, (MEM_W, MEM_W), _tile(batch * n_mem, 512))
        k_exp = (mk.reshape(batch, 1, n_mem, MEM_W) * head_mask[None, :, None, :]).astype(BF16)
        v_exp = (mv.reshape(batch, 1, n_mem, MEM_W) * head_mask[None, :, None, :]).astype(BF16)
        o_mem = _mem_group(q_mem, k_exp, v_exp, batch, seq, _tile(seq, 512))

        x1 = _out_project((o_gla, o_hg, o_rw, o_mem), xc, w_out[l].astype(BF16), row(ln1_g[l]), row(ln1_b[l]), alpha, tm)

        xc = _moe(x1, router_w[l], router_b[l], w_g_all, w_u_all, b_g_all, b_u_all, w_d_all, b_d_all,
                  l * N_EXPERTS, row(ln2_g[l]), row(ln2_b[l]), alpha, tm, tm_exp)
    return xc.reshape(batch, seq, d)
```

```python
import functools

import numpy as np
import jax
import jax.numpy as jnp
from jax import lax
from jax.experimental import pallas as pl
from jax.experimental.pallas import tpu as pltpu

F32 = jnp.float32
BF16 = jnp.bfloat16
HIGHEST = lax.Precision.HIGHEST

D_MODEL = 1024
HEAD_DIM = 64
N_HEADS = 4
GROUP_W = 256
GLA_DK = 32
GLA_QK = 128
GLA_V = 256
GLA_GATE_RANK = 16
GLA_TAU = 16.0
CONV_W = 4
GLA_CONV_C = 2 * GLA_QK + GLA_V
GLA_COLS = 784
GLA_PAD = 896
HG_COLS = 1024
RW_W = 256
RW_DECAY_RANK = 64
RW_A_RANK = 64
RW_GATE_RANK = 128
RW_COLS = 1024
RW_LN_EPS = 64e-5
MEM_W = 256
N_EXPERTS = 32
TOP_K = 4
SWIGLU_ALPHA = 1.702
SWIGLU_LIMIT = 7.0
LN_EPS = 1e-5
RMS_EPS = 1e-6
GATE_FLOOR = 1e-30
LOG2E = 1.4426950408889634
NEG_BIG = -1.0e30

LANES = 128
SUBLANES = 8
SUB_CHUNK = 16
RW_CHUNK = 32
VMEM_LIMIT = 48 * 1024 * 1024


def _cparams(n_axes):
    return pltpu.CompilerParams(dimension_semantics=("arbitrary",) * n_axes, vmem_limit_bytes=VMEM_LIMIT)


def _dot(a, b, precision=None):
    return jnp.dot(a, b, preferred_element_type=F32, precision=precision)


def _dot_nt(a, b):
    return lax.dot_general(a, b, (((1,), (1,)), ((), ())), preferred_element_type=F32)


def _dot_tn(a, b):
    return lax.dot_general(a, b, (((0,), (0,)), ((), ())), preferred_element_type=F32)


def _split_bf16(x, terms):
    parts = []
    for _ in range(terms):
        p = x.astype(BF16)
        parts.append(p)
        x = x - p.astype(F32)
    return parts


def _dot_onehot_rhs(x, e_bf16, terms):
    parts = _split_bf16(x, terms)
    acc = _dot(parts[0], e_bf16)
    for p in parts[1:]:
        acc = acc + _dot(p, e_bf16)
    return acc


def _dot_onehot_lhs(e_bf16, x, terms):
    parts = _split_bf16(x, terms)
    acc = _dot(e_bf16, parts[0])
    for p in parts[1:]:
        acc = acc + _dot(e_bf16, p)
    return acc


def _sigmoid(x):
    return 1.0 / (1.0 + jnp.exp(-x))


def _silu(x):
    return x * _sigmoid(x)


def _softplus(x):
    return jnp.maximum(x, 0.0) + jnp.log(1.0 + jnp.exp(-jnp.abs(x)))


def _const_spec(shape):
    nd = len(shape)
    return pl.BlockSpec(shape, lambda *_: (0,) * nd)


def _proj_kernel(x_ref, w_ref, *out_refs, widths):
    xb = x_ref[...].astype(BF16)
    off = 0
    for o_ref, wd in zip(out_refs, widths):
        o_ref[...] = _dot(xb, w_ref[:, off:off + wd]).astype(o_ref.dtype)
        off += wd


def _project(x2d, w_bf16, widths, tm):
    m, k = x2d.shape
    n = w_bf16.shape[1]
    assert sum(widths) == n and m % tm == 0
    return pl.pallas_call(
        functools.partial(_proj_kernel, widths=tuple(widths)),
        grid=(m // tm,),
        in_specs=[pl.BlockSpec((tm, k), lambda i: (i, 0)), _const_spec((k, n))],
        out_specs=[pl.BlockSpec((tm, wd), lambda i: (i, 0)) for wd in widths],
        out_shape=[jax.ShapeDtypeStruct((m, wd), F32) for wd in widths],
        compiler_params=_cparams(1),
        name="in_proj",
    )(x2d, w_bf16)


def _gla_core(q, k, v, g, lt_ref, ekv_ref, bdt_ref, st_ref):
    tt, ck = q.shape
    cv = v.shape[1]
    nc = tt // SUB_CHUNK
    b = _dot_onehot_lhs(lt_ref[...], g * LOG2E, 3)
    q3 = q.reshape(nc, SUB_CHUNK, ck)
    k3 = k.reshape(nc, SUB_CHUNK, ck)
    v3 = v.reshape(nc, SUB_CHUNK, cv)
    b3 = b.reshape(nc, SUB_CHUNK, ck)
    row = lax.broadcasted_iota(jnp.int32, (nc, SUB_CHUNK, ck), 1)
    ekv = ekv_ref[...]

    o3 = jnp.zeros((nc, SUB_CHUNK, cv), F32)
    for j in range(SUB_CHUNK):
        rel = jnp.where(row >= j, b3 - b3[:, j:j + 1, :], NEG_BIG)
        w = q3 * jnp.exp2(rel) * k3[:, j:j + 1, :]
        s = _dot(w.reshape(tt, ck).astype(BF16), ekv).reshape(nc, SUB_CHUNK, cv)
        o3 = o3 + s * v3[:, j:j + 1, :]

    bend3 = b3[:, SUB_CHUNK - 1:SUB_CHUNK, :]
    qg = (q3 * jnp.exp2(b3)).reshape(tt, ck).astype(BF16)
    kg = (k3 * jnp.exp2(bend3 - b3)).reshape(tt, ck).astype(BF16)
    dec = jnp.exp2(bend3)
    vb = v.astype(BF16)
    bdt = bdt_ref[...]
    st = st_ref[...]
    outs = []
    for c in range(nc):
        sl = slice(c * SUB_CHUNK, (c + 1) * SUB_CHUNK)
        outs.append(_dot_nt(qg[sl], st.astype(BF16)))
        st = st * dec[c] + bdt * _dot_tn(vb[sl], kg[sl])
    st_ref[...] = st
    return o3.reshape(tt, cv) + jnp.concatenate(outs, axis=0)


def _head_rms_gate(o, evv, norm_g, gate):
    ms = _dot((o * o).astype(BF16), evv) * (1.0 / HEAD_DIM)
    return o * lax.rsqrt(ms + RMS_EPS) * norm_g * _silu(gate)


def _gla_kernel(p_ref, conv_ref, w2_ref, gb_ref, ng_ref, lt_ref, ekv_ref, bdt_ref, evv_ref,
                o_ref, xp_ref, st_ref, *, tt):
    t = pl.program_id(1)

    @pl.when(t == 0)
    def _():
        st_ref[...] = jnp.zeros_like(st_ref)
        xp_ref[tt:tt + SUBLANES, :] = jnp.zeros((SUBLANES, GLA_CONV_C), F32)

    xp_ref[0:SUBLANES, :] = xp_ref[tt:tt + SUBLANES, :]
    xp_ref[SUBLANES:tt + SUBLANES, :] = p_ref[:, 0:GLA_CONV_C]
    acc = jnp.zeros((tt, GLA_CONV_C), F32)
    for i in range(CONV_W):
        shift = CONV_W - 1 - i
        acc = acc + xp_ref[SUBLANES - shift:SUBLANES - shift + tt, :] * conv_ref[i:i + 1, :]
    qkv = _silu(acc)
    q = qkv[:, 0:GLA_QK] * (GLA_DK ** -0.5)
    k = qkv[:, GLA_QK:2 * GLA_QK]
    v = qkv[:, 2 * GLA_QK:GLA_CONV_C]
    g_out = p_ref[:, GLA_CONV_C:GLA_CONV_C + GLA_V]
    z = _dot(p_ref[:, GLA_CONV_C + GLA_V:GLA_PAD], w2_ref[...], precision=HIGHEST) + gb_ref[...]
    log_alpha = -_softplus(-z) * (1.0 / GLA_TAU)
    o = _gla_core(q, k, v, log_alpha, lt_ref, ekv_ref, bdt_ref, st_ref)
    o_ref[...] = _head_rms_gate(o, evv_ref[...], ng_ref[...], g_out).astype(o_ref.dtype)


def _hgrn_kernel(p_ref, lb_ref, ng_ref, lt_ref, ekv_ref, bdt_ref, evv_ref, o_ref, st_ref):
    t = pl.program_id(1)

    @pl.when(t == 0)
    def _():
        st_ref[...] = jnp.zeros_like(st_ref)

    hq = p_ref[:, 0:256]
    hf = p_ref[:, 256:512]
    hi = p_ref[:, 512:768]
    hgate = p_ref[:, 768:1024]
    lb = lb_ref[...]
    f_gate = lb + (1.0 - lb) * _sigmoid(hf)
    log_f = jnp.log(jnp.maximum(f_gate, GATE_FLOOR))
    k_in = (1.0 - lb) * _sigmoid(-hf)
    o = _gla_core(_silu(hq), k_in, hi, log_f, lt_ref, ekv_ref, bdt_ref, st_ref)
    o_ref[...] = _head_rms_gate(o, evv_ref[...], ng_ref[...], hgate).astype(o_ref.dtype)


def _head_of(n, width):
    return np.arange(n) // width


def _gla_consts(tt, ck, cv):
    r = np.arange(tt)
    lt = ((r[:, None] // SUB_CHUNK == r[None, :] // SUB_CHUNK) & (r[None, :] <= r[:, None])).astype(np.float32)
    hk = _head_of(ck, ck // N_HEADS)
    hv = _head_of(cv, cv // N_HEADS)
    ekv = (hk[:, None] == hv[None, :]).astype(np.float32)
    evv = (hv[:, None] == hv[None, :]).astype(np.float32)
    return (jnp.asarray(lt, BF16), jnp.asarray(ekv, BF16), jnp.asarray(ekv.T, F32), jnp.asarray(evv, BF16))


def _gla_group(p_gla, conv_w, w2pad, gate_b, norm_g, batch, seq, tt):
    lt, ekv, bdt, evv = _gla_consts(tt, GLA_QK, GLA_V)
    nt = seq // tt
    row_spec = lambda w: pl.BlockSpec((tt, w), lambda b, t: (b * nt + t, 0))
    return pl.pallas_call(
        functools.partial(_gla_kernel, tt=tt),
        grid=(batch, nt),
        in_specs=[row_spec(GLA_PAD), _const_spec(conv_w.shape), _const_spec(w2pad.shape),
                  _const_spec(gate_b.shape), _const_spec(norm_g.shape), _const_spec(lt.shape),
                  _const_spec(ekv.shape), _const_spec(bdt.shape), _const_spec(evv.shape)],
        out_specs=row_spec(GLA_V),
        out_shape=jax.ShapeDtypeStruct((batch * seq, GLA_V), BF16),
        scratch_shapes=[pltpu.VMEM((tt + 2 * SUBLANES, GLA_CONV_C), F32), pltpu.VMEM((GLA_V, GLA_QK), F32)],
        compiler_params=_cparams(2),
        name="gla_group",
    )(p_gla, conv_w, w2pad, gate_b, norm_g, lt, ekv, bdt, evv)


def _hgrn_group(p_hg, lb, norm_g, batch, seq, tt):
    lt, ekv, bdt, evv = _gla_consts(tt, 256, 256)
    nt = seq // tt
    row_spec = lambda w: pl.BlockSpec((tt, w), lambda b, t: (b * nt + t, 0))
    return pl.pallas_call(
        _hgrn_kernel,
        grid=(batch, nt),
        in_specs=[row_spec(HG_COLS), _const_spec(lb.shape), _const_spec(norm_g.shape), _const_spec(lt.shape),
                  _const_spec(ekv.shape), _const_spec(bdt.shape), _const_spec(evv.shape)],
        out_specs=row_spec(256),
        out_shape=jax.ShapeDtypeStruct((batch * seq, 256), BF16),
        scratch_shapes=[pltpu.VMEM((256, 256), F32)],
        compiler_params=_cparams(2),
        name="hgrn_group",
    )(p_hg, lb, norm_g, lt, ekv, bdt, evv)


def _expand_heads(x, hm):
    return jnp.concatenate([x * hm[h:h + 1, :] for h in range(N_HEADS)], axis=0)


def _fold_heads(x_e, c):
    out = x_e[0:c]
    for h in range(1, N_HEADS):
        out = out + x_e[h * c:(h + 1) * c]
    return out


def _rwkv_kernel(p_ref, mu_ref, w0_ref, w2_ref, a0_ref, a2_ref, g2_ref, kk_ref, ka_ref, rk_ref,
                 lnw_ref, lnb_ref, tri_ref, hm_ref, sl_ref, il_ref, eye_ref, bd_ref, evv_ref,
                 o_ref, xp_ref, st_ref, *, tt):
    t = pl.program_id(1)

    @pl.when(t == 0)
    def _():
        st_ref[...] = jnp.zeros_like(st_ref)
        xp_ref[tt:tt + SUBLANES, :] = jnp.zeros((SUBLANES, RW_COLS), F32)

    xp_ref[0:SUBLANES, :] = xp_ref[tt:tt + SUBLANES, :]
    xp_ref[SUBLANES:tt + SUBLANES, :] = p_ref[...]
    cur = p_ref[...]
    prev = xp_ref[SUBLANES - 1:SUBLANES - 1 + tt, :]
    p = cur + (prev - cur) * mu_ref[...]
    rr = p[:, 0:256]
    rk = p[:, 256:512]
    rv = p[:, 512:768]
    lr = p[:, 768:896]
    g_lr = p[:, 896:1024]
    w_log = -_softplus(-(w0_ref[...] + _dot(jnp.tanh(lr), w2_ref[...], precision=HIGHEST))) - 0.5
    lw = -jnp.exp(w_log)
    a = _sigmoid(a0_ref[...] + _dot(lr, a2_ref[...], precision=HIGHEST))
    g = _dot(_sigmoid(g_lr).astype(BF16), g2_ref[...])
    evv = evv_ref[...]
    kk = rk * kk_ref[...]
    kk_sq = _dot_onehot_rhs(kk * kk, evv, 2)
    kk = kk / jnp.maximum(jnp.sqrt(kk_sq), 1e-12)
    hk = rk * (1.0 + (a - 1.0) * ka_ref[...])
    a_vec = -kk
    b_vec = kk * a

    hm = hm_ref[...]
    sl_m = sl_ref[...]
    il_m = il_ref[...]
    bd = bd_ref[...]
    eye = eye_ref[...]
    c = RW_CHUNK
    nc = tt // c
    n = N_HEADS * c

    cs = _dot_onehot_lhs(tri_ref[...], lw, 3)
    igam = jnp.exp(-cs)
    at = a_vec * jnp.exp(cs - lw)
    bh = b_vec * igam
    kh = hk * igam
    rt = rr * jnp.exp(cs)
    gam_end = jnp.exp(cs.reshape(nc, c, RW_W)[:, c - 1:c, :])
    bhg = (bh.reshape(nc, c, RW_W) * gam_end).reshape(tt, RW_W)
    khg = (kh.reshape(nc, c, RW_W) * gam_end).reshape(tt, RW_W)

    def expand(x):
        return _expand_heads(x, hm).astype(BF16)

    chunks = [slice(ci * c, (ci + 1) * c) for ci in range(nc)]
    at_e = [expand(at[s_]) for s_ in chunks]
    v_e = [expand(rv[s_]) for s_ in chunks]
    l_ab, l_ak_rk, l_rb = [], [], []
    for ci, s_ in enumerate(chunks):
        lhs = jnp.concatenate([at_e[ci], expand(rt[s_])], axis=0)
        rhs = jnp.concatenate([expand(bh[s_]), expand(kh[s_])], axis=0)
        l_all = _dot_nt(lhs, rhs)
        l_ab.append(l_all[0:n, 0:n] * sl_m)
        l_rb.append((l_all[n:2 * n, 0:n] * il_m).astype(BF16))
        l_ak_rk.append(jnp.concatenate([l_all[0:n, n:2 * n] * sl_m, l_all[n:2 * n, n:2 * n] * il_m],
                                       axis=0).astype(BF16))
    tms = [eye + l for l in l_ab]
    pws = l_ab
    for _ in range(int(np.log2(c)) - 1):
        pws = [_dot(p.astype(BF16), p.astype(BF16)) for p in pws]
        tms = [tm + _dot(tm.astype(BF16), p.astype(BF16)) for tm, p in zip(tms, pws)]
    w_c, uv_c, yv_c = [], [], []
    for ci in range(nc):
        lv = _dot(l_ak_rk[ci], v_e[ci])
        tw = _dot(tms[ci].astype(BF16), jnp.concatenate([at_e[ci], lv[0:n].astype(BF16)], axis=1))
        w_c.append(_fold_heads(tw[:, 0:RW_W], c))
        uv_c.append(_fold_heads(tw[:, RW_W:2 * RW_W], c))
        yv_c.append(_fold_heads(lv[n:2 * n], c))

    st = st_ref[...]
    ys = []
    for ci, s_ in enumerate(chunks):
        wr = _dot_nt(jnp.concatenate([w_c[ci], rt[s_]], axis=0).astype(BF16), st.astype(BF16))
        u = wr[0:c] + uv_c[ci]
        ys.append(wr[c:2 * c] + _fold_heads(_dot(l_rb[ci], expand(u)), c) + yv_c[ci])
        lhs = jnp.concatenate([u, rv[s_]], axis=0).astype(BF16)
        rhs = jnp.concatenate([bhg[s_], khg[s_]], axis=0).astype(BF16)
        st = st * gam_end[ci] + bd * _dot_tn(lhs, rhs)
    st_ref[...] = st
    y = jnp.concatenate(ys, axis=0)

    inv_n = 1.0 / HEAD_DIM
    mu = _dot(y.astype(BF16), evv) * inv_n
    yc = y - mu
    var = _dot((yc * yc).astype(BF16), evv) * inv_n
    y = yc * lax.rsqrt(var + RW_LN_EPS) * lnw_ref[...] + lnb_ref[...]
    bonus = _dot_onehot_rhs(rr * hk * rk_ref[...], evv, 2) * rv
    o_ref[...] = ((y + bonus) * g).astype(o_ref.dtype)


def _rwkv_consts(tt):
    c = RW_CHUNK
    n = N_HEADS * c
    r = np.arange(n)
    same = (r[:, None] // c) == (r[None, :] // c)
    sl = same & ((r[None, :] % c) < (r[:, None] % c))
    il = same & ((r[None, :] % c) <= (r[:, None] % c))
    q = np.arange(tt)
    tri = ((q[:, None] // c == q[None, :] // c) & (q[None, :] <= q[:, None])).astype(np.float32)
    hv = _head_of(RW_W, HEAD_DIM)
    hm = (hv[None, :] == np.arange(N_HEADS)[:, None]).astype(np.float32)
    bd = (hv[:, None] == hv[None, :]).astype(np.float32)
    return (jnp.asarray(tri, BF16), jnp.asarray(hm, F32), jnp.asarray(sl, F32), jnp.asarray(il, F32),
            jnp.asarray(np.eye(n), F32), jnp.asarray(bd, F32), jnp.asarray(bd, BF16))


def _rwkv_group(p_rw, params, batch, seq, tt):
    consts = _rwkv_consts(tt)
    nt = seq // tt
    row_spec = lambda w: pl.BlockSpec((tt, w), lambda b, t: (b * nt + t, 0))
    ops = tuple(params) + consts
    return pl.pallas_call(
        functools.partial(_rwkv_kernel, tt=tt),
        grid=(batch, nt),
        in_specs=[row_spec(RW_COLS)] + [_const_spec(o.shape) for o in ops],
        out_specs=row_spec(RW_W),
        out_shape=jax.ShapeDtypeStruct((batch * seq, RW_W), BF16),
        scratch_shapes=[pltpu.VMEM((tt + 2 * SUBLANES, RW_COLS), F32), pltpu.VMEM((RW_W, RW_W), F32)],
        compiler_params=_cparams(2),
        name="rwkv_group",
    )(p_rw, *ops)


def _mem_kernel(q_ref, k_ref, v_ref, o_ref):
    q = (q_ref[...] * (HEAD_DIM ** -0.5)).astype(BF16)
    acc = jnp.zeros(o_ref.shape, F32)
    for h in range(N_HEADS):
        s = _dot_nt(q, k_ref[0, h])
        e = jnp.exp(s - jnp.max(s, axis=-1, keepdims=True))
        l = jnp.sum(e, axis=-1, keepdims=True)
        acc = acc + _dot(e.astype(BF16), v_ref[0, h]) / l
    o_ref[...] = acc.astype(o_ref.dtype)


def _mem_group(q_mem, k_exp, v_exp, batch, seq, tt):
    nt = seq // tt
    n_mem = k_exp.shape[2]
    kv_spec = pl.BlockSpec((1, N_HEADS, n_mem, MEM_W), lambda b, t: (b, 0, 0, 0))
    row_spec = pl.BlockSpec((tt, MEM_W), lambda b, t: (b * nt + t, 0))
    return pl.pallas_call(
        _mem_kernel,
        grid=(batch, nt),
        in_specs=[row_spec, kv_spec, kv_spec],
        out_specs=row_spec,
        out_shape=jax.ShapeDtypeStruct((batch * seq, MEM_W), BF16),
        compiler_params=_cparams(2),
        name="mem_attn",
    )(q_mem, k_exp, v_exp)


def _layer_norm(z, g, b):
    mu = jnp.mean(z, axis=-1, keepdims=True)
    zc = z - mu
    var = jnp.mean(zc * zc, axis=-1, keepdims=True)
    return zc * lax.rsqrt(var + LN_EPS) * g + b


def _outproj_kernel(oa_ref, ob_ref, oc_ref, od_ref, x_ref, w_ref, g_ref, b_ref, y_ref, *, alpha):
    h = _dot(oa_ref[...], w_ref[0:256, :])
    h = h + _dot(ob_ref[...], w_ref[256:512, :])
    h = h + _dot(oc_ref[...], w_ref[512:768, :])
    h = h + _dot(od_ref[...], w_ref[768:1024, :])
    y_ref[...] = _layer_norm(alpha * x_ref[...] + h, g_ref[...], b_ref[...])


def _out_project(groups, x2d, w_out, ln_g, ln_b, alpha, tm):
    m = x2d.shape[0]
    g_spec = pl.BlockSpec((tm, GROUP_W), lambda i: (i, 0))
    x_spec = pl.BlockSpec((tm, D_MODEL), lambda i: (i, 0))
    return pl.pallas_call(
        functools.partial(_outproj_kernel, alpha=alpha),
        grid=(m // tm,),
        in_specs=[g_spec] * 4 + [x_spec, _const_spec(w_out.shape), _const_spec(ln_g.shape), _const_spec(ln_b.shape)],
        out_specs=x_spec,
        out_shape=jax.ShapeDtypeStruct((m, D_MODEL), F32),
        compiler_params=_cparams(1),
        name="out_proj_ln",
    )(*groups, x2d, w_out, ln_g, ln_b)


def _router_kernel(x_ref, w_ref, b_ref, tri_ref, idx_ref, gate_ref, rank_ref, cnt_ref, carry_ref):
    i = pl.program_id(0)

    @pl.when(i == 0)
    def _():
        carry_ref[...] = jnp.zeros_like(carry_ref)

    tm = x_ref.shape[0]
    lane = lax.broadcasted_iota(jnp.int32, (tm, LANES), 1)
    logits = _dot(x_ref[...], w_ref[...], precision=HIGHEST) + b_ref[...]
    neg = jnp.float32(-3.0e38)
    work = jnp.where(lane < N_EXPERTS, logits, neg)
    vals, idxs = [], []
    sel = jnp.zeros((tm, LANES), F32)
    for _ in range(TOP_K):
        m = jnp.max(work, axis=-1, keepdims=True)
        idx = jnp.min(jnp.where(work == m, lane, LANES), axis=-1, keepdims=True)
        hit = lane == idx
        sel = jnp.where(hit, 1.0, sel)
        work = jnp.where(hit, neg, work)
        vals.append(m)
        idxs.append(idx)
    es = [jnp.exp(v - vals[0]) for v in vals]
    den = es[0] + es[1] + es[2] + es[3]
    incl = _dot(tri_ref[...], sel.astype(BF16))
    rank_dense = incl - sel + carry_ref[...]
    carry_ref[...] = carry_ref[...] + incl[tm - 1:tm, :]
    cnt_ref[...] = carry_ref[...]
    idx_out = jnp.zeros((tm, LANES), jnp.int32)
    gate_out = jnp.zeros((tm, LANES), F32)
    rank_out = jnp.zeros((tm, LANES), F32)
    for kk in range(TOP_K):
        rk = jnp.sum(jnp.where(lane == idxs[kk], rank_dense, 0.0), axis=-1, keepdims=True)
        here = lane == kk
        idx_out = jnp.where(here, idxs[kk], idx_out)
        gate_out = jnp.where(here, es[kk] / den, gate_out)
        rank_out = jnp.where(here, rk, rank_out)
    idx_ref[...] = idx_out
    gate_ref[...] = gate_out
    rank_ref[...] = rank_out


def _route(x2d, rw_pad, rb_pad, tm):
    m = x2d.shape[0]
    tri = jnp.asarray(np.tril(np.ones((tm, tm), np.float32)), BF16)
    row = pl.BlockSpec((tm, LANES), lambda i: (i, 0))
    return pl.pallas_call(
        _router_kernel,
        grid=(m // tm,),
        in_specs=[pl.BlockSpec((tm, D_MODEL), lambda i: (i, 0)), _const_spec(rw_pad.shape),
                  _const_spec(rb_pad.shape), _const_spec(tri.shape)],
        out_specs=[row, row, row, _const_spec((1, LANES))],
        out_shape=[jax.ShapeDtypeStruct((m, LANES), jnp.int32), jax.ShapeDtypeStruct((m, LANES), F32),
                   jax.ShapeDtypeStruct((m, LANES), F32), jax.ShapeDtypeStruct((1, LANES), F32)],
        scratch_shapes=[pltpu.VMEM((1, LANES), F32)],
        compiler_params=_cparams(1),
        name="router",
    )(x2d, rw_pad, rb_pad, tri)


DEINT_W = 2 * LANES


def _deinterleave_kernel(w_ref, perm_ref, wg_ref, wu_ref):
    perm = perm_ref[...]
    for blk in range(w_ref.shape[2] // DEINT_W):
        wb = w_ref[0, :, blk * DEINT_W:(blk + 1) * DEINT_W].astype(BF16)
        r = _dot(wb, perm)
        wg_ref[0, :, blk * LANES:(blk + 1) * LANES] = r[:, 0:LANES].astype(BF16)
        wu_ref[0, :, blk * LANES:(blk + 1) * LANES] = r[:, LANES:DEINT_W].astype(BF16)


def _deinterleave_gate_up(w_gu, tk):
    n, k, f2 = w_gu.shape
    f = f2 // 2
    j = np.arange(DEINT_W)
    src = np.where(j < LANES, 2 * j, 2 * (j - LANES) + 1)
    perm = np.zeros((DEINT_W, DEINT_W), np.float32)
    perm[src, j] = 1.0
    perm = jnp.asarray(perm, BF16)
    out_spec = pl.BlockSpec((1, tk, f), lambda e, i: (e, i, 0))
    return pl.pallas_call(
        _deinterleave_kernel,
        grid=(n, k // tk),
        in_specs=[pl.BlockSpec((1, tk, f2), lambda e, i: (e, i, 0)), _const_spec(perm.shape)],
        out_specs=[out_spec, out_spec],
        out_shape=[jax.ShapeDtypeStruct((n, k, f), BF16)] * 2,
        compiler_params=_cparams(2),
        name="deinterleave_gate_up",
    )(w_gu, perm)


def _expert_kernel(be_ref, nu_ref, x_ref, wg_ref, wu_ref, bg_ref, bu_ref, wd_ref, bd_ref, y_ref):
    i = pl.program_id(0)

    @pl.when(i < nu_ref[0])
    def _():
        x = x_ref[...]
        hg = _dot(x, wg_ref[0]) + bg_ref[0]
        hu = _dot(x, wu_ref[0]) + bu_ref[0]
        gate = jnp.minimum(hg, SWIGLU_LIMIT)
        up = jnp.clip(hu, -SWIGLU_LIMIT, SWIGLU_LIMIT)
        glu = gate * _sigmoid(gate * SWIGLU_ALPHA)
        act = ((up + 1.0) * glu).astype(BF16)
        y_ref[...] = (_dot(act, wd_ref[0]) + bd_ref[0]).astype(y_ref.dtype)

    @pl.when(i >= nu_ref[0])
    def _():
        y_ref[...] = jnp.zeros_like(y_ref)


def _expert_ffn(xs, block_e, n_used, w_g, w_u, b_g, b_u, w_d, b_d, tm):
    p = xs.shape[0]
    nb = p // tm
    f = w_g.shape[2]
    w_spec = lambda s: pl.BlockSpec((1,) + s, lambda i, be, nu: (be[i], 0, 0))
    grid_spec = pltpu.PrefetchScalarGridSpec(
        num_scalar_prefetch=2,
        grid=(nb,),
        in_specs=[pl.BlockSpec((tm, D_MODEL), lambda i, be, nu: (i, 0)),
                  w_spec((D_MODEL, f)), w_spec((D_MODEL, f)), w_spec((1, f)), w_spec((1, f)),
                  w_spec((f, D_MODEL)), w_spec((1, D_MODEL))],
        out_specs=pl.BlockSpec((tm, D_MODEL), lambda i, be, nu: (i, 0)),
    )
    return pl.pallas_call(
        _expert_kernel,
        grid_spec=grid_spec,
        out_shape=jax.ShapeDtypeStruct((p, D_MODEL), F32),
        compiler_params=_cparams(1),
        name="expert_ffn",
    )(block_e, n_used, xs, w_g, w_u, b_g, b_u, w_d, b_d)


def _combine_kernel(yg_ref, gate_ref, x_ref, g_ref, b_ref, o_ref, *, alpha):
    tm = x_ref.shape[0]
    lane = lax.broadcasted_iota(jnp.int32, (tm, LANES), 1)
    gd = gate_ref[...]
    m = jnp.zeros((tm, D_MODEL), F32)
    for kk in range(TOP_K):
        gk = jnp.sum(jnp.where(lane == kk, gd, 0.0), axis=-1, keepdims=True)
        m = m + gk * yg_ref[kk].astype(F32)
    o_ref[...] = _layer_norm(alpha * x_ref[...] + m, g_ref[...], b_ref[...])


def _combine(yg, gates_dense, x2d, ln_g, ln_b, alpha, tm):
    m = x2d.shape[0]
    x_spec = pl.BlockSpec((tm, D_MODEL), lambda i: (i, 0))
    return pl.pallas_call(
        functools.partial(_combine_kernel, alpha=alpha),
        grid=(m // tm,),
        in_specs=[pl.BlockSpec((TOP_K, tm, D_MODEL), lambda i: (0, i, 0)),
                  pl.BlockSpec((tm, LANES), lambda i: (i, 0)), x_spec,
                  _const_spec(ln_g.shape), _const_spec(ln_b.shape)],
        out_specs=x_spec,
        out_shape=jax.ShapeDtypeStruct((m, D_MODEL), F32),
        compiler_params=_cparams(1),
        name="combine_ln",
    )(yg, gates_dense, x2d, ln_g, ln_b)


def _tile(n, pref):
    t = min(n, pref)
    assert n % t == 0
    return t


def _pad_rows(w, rows, at):
    out = jnp.zeros((rows, w.shape[1]), w.dtype)
    return out.at[at:at + w.shape[0]].set(w)


def _moe(x1, router_w, router_b, w_g, w_u, b_g, b_u, w_d, b_d, expert_base, ln_g, ln_b, alpha, tm_tok, tm_exp):
    n_tok = x1.shape[0]
    rw_pad = jnp.zeros((D_MODEL, LANES), F32).at[:, :N_EXPERTS].set(router_w)
    rb_pad = jnp.zeros((1, LANES), F32).at[0, :N_EXPERTS].set(router_b)
    idx_d, gate_d, rank_d, counts_d = _route(x1, rw_pad, rb_pad, tm_tok)
    top_idx = idx_d[:, :TOP_K]
    rank = rank_d[:, :TOP_K].astype(jnp.int32)
    counts = counts_d[0, :N_EXPERTS].astype(jnp.int32)
    padded = (counts + tm_exp - 1) // tm_exp * tm_exp
    pad_end = jnp.cumsum(padded)
    pad_start = pad_end - padded
    dest = pad_start[top_idx] + rank
    n_blocks = -(-(n_tok * TOP_K) // tm_exp) + N_EXPERTS
    n_slots = n_blocks * tm_exp
    n_used = (pad_end[-1] // tm_exp).astype(jnp.int32)
    blk = jnp.arange(n_blocks, dtype=jnp.int32)
    blk_row = jnp.minimum(blk, n_used - 1) * tm_exp
    block_e = jnp.sum((pad_end[None, :] <= blk_row[:, None]).astype(jnp.int32), axis=1)
    block_e = jnp.minimum(block_e, N_EXPERTS - 1) + expert_base
    tok = jnp.broadcast_to(jnp.arange(n_tok, dtype=jnp.int32)[:, None], (n_tok, TOP_K))
    slot_tok = jnp.full((n_slots,), n_tok, jnp.int32).at[dest.reshape(-1)].set(tok.reshape(-1))
    x_pad = jnp.concatenate([x1.astype(BF16), jnp.zeros((1, D_MODEL), BF16)], axis=0)
    xs = x_pad[slot_tok]
    yb = _expert_ffn(xs, block_e, n_used.reshape(1), w_g, w_u, b_g, b_u, w_d, b_d, tm_exp)
    yg = yb[dest.T]
    return _combine(yg, gate_d, x1, ln_g, ln_b, alpha, tm_tok)


def kernel(x, mem, w_in, gla_conv, gla_gate_w2, gla_gate_b, gla_norm_g, hg_lower_bound, hg_norm_g, rw_mu, rw_w0, rw_w2, rw_a0, rw_a2, rw_g2, rw_k_k, rw_k_a, rw_r_k, rw_ln_w, rw_ln_b, w_mem_k, w_mem_v, w_out, ln1_g, ln1_b, router_w, router_b, w_gu, b_gu, w_down, b_down, ln2_g, ln2_b):
    batch, seq, d = x.shape
    depth = w_in.shape[0]
    n_mem = mem.shape[1]
    n_tok = batch * seq
    alpha = float((2.0 * depth) ** 0.25)
    tt = _tile(seq, 256)
    tm = _tile(n_tok, 512)
    tm_exp = 512

    lb = jax.nn.softmax(hg_lower_bound.astype(F32), axis=0)
    lb = jnp.cumsum(lb, axis=0) - lb[0]

    hv = _head_of(MEM_W, HEAD_DIM)
    head_mask = jnp.asarray((hv[None, :] == np.arange(N_HEADS)[:, None]).astype(np.float32))

    c0 = GLA_COLS
    c1 = c0 + HG_COLS
    c2 = c1 + RW_COLS
    widths = (GLA_PAD, HG_COLS, RW_COLS, MEM_W)
    xc = x.reshape(n_tok, d)
    mem2d = mem.reshape(batch * n_mem, d)
    row = lambda v: v.reshape(1, -1).astype(F32)

    n_le = depth * N_EXPERTS
    f2 = w_gu.shape[-1]
    w_g_all, w_u_all = _deinterleave_gate_up(w_gu.reshape(n_le, d, f2), 512)
    b_gu_all = b_gu.reshape(n_le, 1, f2)
    b_g_all = b_gu_all[:, :, 0::2]
    b_u_all = b_gu_all[:, :, 1::2]
    w_d_all = w_down.reshape(n_le, w_down.shape[2], d).astype(BF16)
    b_d_all = b_down.reshape(n_le, 1, d)
    for l in range(depth):
        wl = w_in[l]
        w_gla = jnp.concatenate([wl[:, 0:GLA_CONV_C], wl[:, GLA_CONV_C + GLA_GATE_RANK:c0],
                                 wl[:, GLA_CONV_C:GLA_CONV_C + GLA_GATE_RANK],
                                 jnp.zeros((d, GLA_PAD - GLA_COLS), F32)], axis=1)
        w_all = jnp.concatenate([w_gla, wl[:, c0:]], axis=1).astype(BF16)
        p_gla, p_hg, p_rw, q_mem = _project(xc, w_all, widths, tm)

        w2pad = _pad_rows(gla_gate_w2[l], GLA_PAD - GLA_CONV_C - GLA_V, 0)
        o_gla = _gla_group(p_gla, gla_conv[l], w2pad, row(gla_gate_b[l]), row(gla_norm_g[l]), batch, seq, tt)
        o_hg = _hgrn_group(p_hg, row(lb[l]), row(hg_norm_g[l]), batch, seq, tt)
        rw_params = (row(rw_mu[l]), row(rw_w0[l]), _pad_rows(rw_w2[l], LANES, 0), row(rw_a0[l]),
                     _pad_rows(rw_a2[l], LANES, RW_DECAY_RANK), rw_g2[l].astype(BF16), row(rw_k_k[l]),
                     row(rw_k_a[l]), row(rw_r_k[l]), row(rw_ln_w[l]), row(rw_ln_b[l]))
        o_rw = _rwkv_group(p_rw, rw_params, batch, seq, tt)

        w_kv = jnp.concatenate([w_mem_k[l], w_mem_v[l]], axis=1).astype(BF16)
        mk, mv = _project(mem2d, w_kv, (MEM_W, MEM_W), _tile(batch * n_mem, 512))
        k_exp = (mk.reshape(batch, 1, n_mem, MEM_W) * head_mask[None, :, None, :]).astype(BF16)
        v_exp = (mv.reshape(batch, 1, n_mem, MEM_W) * head_mask[None, :, None, :]).astype(BF16)
        o_mem = _mem_group(q_mem, k_exp, v_exp, batch, seq, _tile(seq, 512))

        x1 = _out_project((o_gla, o_hg, o_rw, o_mem), xc, w_out[l].astype(BF16), row(ln1_g[l]), row(ln1_b[l]), alpha, tm)

        n_part = 2 if n_tok % (2 * tm) == 0 else 1
        part = n_tok // n_part
        xc = jnp.concatenate(
            [_moe(x1[i * part:(i + 1) * part], router_w[l], router_b[l], w_g_all, w_u_all, b_g_all, b_u_all,
                  w_d_all, b_d_all, l * N_EXPERTS, row(ln2_g[l]), row(ln2_b[l]), alpha, tm, tm_exp)
             for i in range(n_part)], axis=0)
    return xc.reshape(batch, seq, d)
```
